```python
import jax, jax.numpy as jnp
from jax import lax
import numpy as np

D_MODEL = 1024
BATCH = 16
SEQ = 256
DEPTH = 2
DEC_BATCH = 8
DEC_SEQ = 4096
PAST_LEN = 256

GRID_W = 64
HEAD_DIM = 64
N_Q_HEADS = 12
N_KV_HEADS = 4
KV_GROUP = N_Q_HEADS // N_KV_HEADS
Q_WIDTH = N_Q_HEADS * HEAD_DIM
KV_WIDTH = N_KV_HEADS * HEAD_DIM
F_GROUPS = 4
F_GROUP_DIM = 64
F_WIDTH = F_GROUPS * F_GROUP_DIM
MIX_WIDTH = F_WIDTH + Q_WIDTH
IN_WIDTH = F_WIDTH + Q_WIDTH + 2 * KV_WIDTH
ROPE_AXIS_DIM = HEAD_DIM // 2
ROPE_THETA = 10000.0
Q_BLOCK = 128
N_EXPERT_GROUPS = 4
EXPERTS_PER_GROUP = 4
N_EXPERTS = N_EXPERT_GROUPS * EXPERTS_PER_GROUP
TOP_K_IN_GROUP = 2
D_EXPERT = D_MODEL // 2
N_MOD = 6
EPS = 1e-6

kernel_name = 'hybrid_fourier_gqa_hmoe_diffusion_step'


def rms_norm(x, g):
    xf = x.astype(jnp.float32)
    y = xf * lax.rsqrt(jnp.mean(xf * xf, axis=-1, keepdims=True) + EPS)
    return y.astype(x.dtype) * g


def _rotate(xh, ang):
    cos = jnp.cos(ang)[None, :, None, :].astype(xh.dtype)
    sin = jnp.sin(ang)[None, :, None, :].astype(xh.dtype)
    x1, x2 = jnp.split(xh, 2, axis=-1)
    return jnp.concatenate([x1 * cos - x2 * sin, x2 * cos + x1 * sin], axis=-1)


def axial_rope(x):
    n = x.shape[1]
    rows = n // GRID_W
    row = jnp.repeat(jnp.arange(rows, dtype=jnp.float32), GRID_W)
    col = jnp.tile(jnp.arange(GRID_W, dtype=jnp.float32), rows)
    inv = ROPE_THETA ** (-jnp.arange(0, ROPE_AXIS_DIM, 2, dtype=jnp.float32) / ROPE_AXIS_DIM)
    return jnp.concatenate([_rotate(x[..., :ROPE_AXIS_DIM], row[:, None] * inv),
                            _rotate(x[..., ROPE_AXIS_DIM:], col[:, None] * inv)], axis=-1)


def fourier_mix(f_in):
    b, n, _ = f_in.shape
    z = f_in.reshape(b, n, F_GROUPS, F_GROUP_DIM).astype(jnp.float32)
    zr = jnp.real(jnp.fft.fft2(z, axes=(1, 3), norm='ortho'))
    return zr.reshape(b, n, F_WIDTH).astype(f_in.dtype)


def block_attention(q, k, v):
    b, n = q.shape[:2]
    nb = n // Q_BLOCK
    qb = q.reshape(b, nb, Q_BLOCK, N_KV_HEADS, KV_GROUP, HEAD_DIM).transpose(1, 0, 2, 3, 4, 5)
    scale = HEAD_DIM ** -0.5

    def one_block(qblk):
        s = jnp.einsum('bqkgd,bskd->bkgqs', qblk, k).astype(jnp.float32) * scale
        p = jax.nn.softmax(s, axis=-1).astype(v.dtype)
        return jnp.einsum('bkgqs,bskd->bqkgd', p, v)

    o = lax.map(one_block, qb)
    return o.transpose(1, 0, 2, 3, 4, 5).reshape(b, n, Q_WIDTH)


def hier_moe(h, w_rg, b_rg, w_re, b_re, w_gate, w_up, w_down):
    b, n, d = h.shape
    t = h.reshape(-1, d)
    g_prob = jax.nn.softmax((t @ w_rg).astype(jnp.float32) + b_rg, axis=-1)
    g_w, g_idx = lax.top_k(g_prob, 1)
    e_logits = ((t @ w_re).astype(jnp.float32) + b_re).reshape(-1, N_EXPERT_GROUPS, EXPERTS_PER_GROUP)
    g_onehot = jax.nn.one_hot(g_idx[:, 0], N_EXPERT_GROUPS, dtype=jnp.float32)
    e_sel = jnp.einsum('tg,tge->te', g_onehot, e_logits)
    e_prob = jax.nn.softmax(e_sel, axis=-1)
    e_w, e_idx = lax.top_k(e_prob, TOP_K_IN_GROUP)
    e_w = e_w / jnp.sum(e_w, axis=-1, keepdims=True)
    weights = g_w * e_w
    expert_ids = g_idx * EXPERTS_PER_GROUP + e_idx
    combine = jnp.sum(jax.nn.one_hot(expert_ids, N_EXPERTS, dtype=jnp.float32) * weights[..., None],
                      axis=1).astype(t.dtype)
    y = jnp.zeros_like(t)
    for e in range(N_EXPERTS):
        hid = jax.nn.silu(t @ w_gate[e]) * (t @ w_up[e])
        y = y + combine[:, e:e + 1] * (hid @ w_down[e])
    return y.reshape(b, n, d)


def trunk_layer(x, mod, n1, n2, w_in, w_f, q_g, k_g, w_o, w_rg, b_rg, w_re, b_re, w_gate, w_up, w_down,
                ctx_k=None, ctx_v=None):
    shift1, scale1, gate1, shift2, scale2, gate2 = jnp.split(mod, N_MOD, axis=-1)
    b, n = x.shape[:2]
    h = rms_norm(x, n1) * (1 + scale1) + shift1
    p = h @ w_in
    f_in, q, k, v = jnp.split(p, [F_WIDTH, F_WIDTH + Q_WIDTH, F_WIDTH + Q_WIDTH + KV_WIDTH], axis=-1)
    f_out = fourier_mix(f_in) @ w_f
    q = rms_norm(q.reshape(b, n, N_Q_HEADS, HEAD_DIM), q_g)
    k = rms_norm(k.reshape(b, n, N_KV_HEADS, HEAD_DIM), k_g)
    v = v.reshape(b, n, N_KV_HEADS, HEAD_DIM)
    if ctx_k is None:
        keys, vals = k, v
    else:
        q = axial_rope(q)
        keys = jnp.concatenate([ctx_k, axial_rope(k)], axis=1)
        vals = jnp.concatenate([ctx_v, v], axis=1)
    a_out = block_attention(q.reshape(b, n, N_KV_HEADS, KV_GROUP, HEAD_DIM), keys, vals)
    x = x + gate1 * (jnp.concatenate([f_out, a_out], axis=-1) @ w_o)
    h2 = rms_norm(x, n2) * (1 + scale2) + shift2
    x = x + gate2 * hier_moe(h2, w_rg, b_rg, w_re, b_re, w_gate, w_up, w_down)
    return x, k, v


def setup_inputs(seed: int = 0) -> dict:
    key = jax.random.key(seed)
    ks = jax.random.split(key, 24)
    f32 = jnp.float32
    nrm = lambda k, s, sc: jax.random.normal(k, s, f32) * sc
    cache_shape = (DEC_BATCH, DEPTH, PAST_LEN, N_KV_HEADS, HEAD_DIM)
    return {
        'x_prompt': nrm(ks[0], (BATCH, SEQ, D_MODEL), 1.0),
        'x_sample': nrm(ks[1], (DEC_BATCH, DEC_SEQ, D_MODEL), 1.0),
        'cache_k': nrm(ks[2], cache_shape, 1.0),
        'cache_v': nrm(ks[3], cache_shape, 1.0),
        'c': nrm(ks[4], (DEC_BATCH, D_MODEL), 1.0),
        'c_ctx': nrm(ks[5], (D_MODEL,), 1.0),
        'norm1_g': 1.0 + nrm(ks[6], (DEPTH, D_MODEL), 0.01),
        'norm2_g': 1.0 + nrm(ks[7], (DEPTH, D_MODEL), 0.01),
        'w_mod': nrm(ks[8], (DEPTH, D_MODEL, N_MOD * D_MODEL), 0.5 * D_MODEL ** -0.5),
        'b_mod': nrm(ks[9], (DEPTH, N_MOD * D_MODEL), 0.01),
        'w_in': nrm(ks[10], (DEPTH, D_MODEL, IN_WIDTH), D_MODEL ** -0.5),
        'w_fourier': nrm(ks[11], (DEPTH, F_WIDTH, F_WIDTH), F_WIDTH ** -0.5),
        'q_norm_g': 1.0 + nrm(ks[12], (DEPTH, HEAD_DIM), 0.01),
        'k_norm_g': 1.0 + nrm(ks[13], (DEPTH, HEAD_DIM), 0.01),
        'w_out': nrm(ks[14], (DEPTH, MIX_WIDTH, D_MODEL), MIX_WIDTH ** -0.5),
        'w_router_group': nrm(ks[15], (DEPTH, D_MODEL, N_EXPERT_GROUPS), D_MODEL ** -0.5),
        'b_router_group': nrm(ks[16], (DEPTH, N_EXPERT_GROUPS), 0.01),
        'w_router_expert': nrm(ks[17], (DEPTH, D_MODEL, N_EXPERTS), D_MODEL ** -0.5),
        'b_router_expert': nrm(ks[18], (DEPTH, N_EXPERTS), 0.01),
        'w_gate': nrm(ks[19], (DEPTH, N_EXPERTS, D_MODEL, D_EXPERT), D_MODEL ** -0.5),
        'w_up': nrm(ks[20], (DEPTH, N_EXPERTS, D_MODEL, D_EXPERT), D_MODEL ** -0.5),
        'w_down': nrm(ks[21], (DEPTH, N_EXPERTS, D_EXPERT, D_MODEL), D_EXPERT ** -0.5),
    }


def reference(x_prompt, x_sample, cache_k, cache_v, c, c_ctx, norm1_g, norm2_g, w_mod, b_mod, w_in,
              w_fourier, q_norm_g, k_norm_g, w_out, w_router_group, b_router_group, w_router_expert,
              b_router_expert, w_gate, w_up, w_down):
    y_prompt = x_prompt
    ks_list, vs_list = [], []
    for l in range(DEPTH):
        mod_ctx = (jax.nn.silu(c_ctx) @ w_mod[l] + b_mod[l])[None, None, :]
        y_prompt, k_l, v_l = trunk_layer(
            y_prompt, mod_ctx, norm1_g[l], norm2_g[l], w_in[l], w_fourier[l], q_norm_g[l], k_norm_g[l],
            w_out[l], w_router_group[l], b_router_group[l], w_router_expert[l], b_router_expert[l],
            w_gate[l], w_up[l], w_down[l])
        ks_list.append(k_l)
        vs_list.append(v_l)
    new_cache_k = jnp.stack(ks_list, axis=1)
    new_cache_v = jnp.stack(vs_list, axis=1)

    y_sample = x_sample
    for l in range(DEPTH):
        mod_lat = (jax.nn.silu(c) @ w_mod[l] + b_mod[l])[:, None, :]
        y_sample, _, _ = trunk_layer(
            y_sample, mod_lat, norm1_g[l], norm2_g[l], w_in[l], w_fourier[l], q_norm_g[l], k_norm_g[l],
            w_out[l], w_router_group[l], b_router_group[l], w_router_expert[l], b_router_expert[l],
            w_gate[l], w_up[l], w_down[l], ctx_k=cache_k[:, l], ctx_v=cache_v[:, l])

    return (y_prompt, y_sample, new_cache_k, new_cache_v)
```

```python
import functools

import numpy as np
import jax
import jax.numpy as jnp
from jax import lax
from jax.experimental import pallas as pl
from jax.experimental.pallas import tpu as pltpu

F32 = jnp.float32
BF16 = jnp.bfloat16

HEAD_DIM = 64
N_Q_HEADS = 12
N_KV_HEADS = 4
KV_GROUP = N_Q_HEADS // N_KV_HEADS
Q_WIDTH = N_Q_HEADS * HEAD_DIM
KV_WIDTH = N_KV_HEADS * HEAD_DIM
F_GROUPS = 4
F_GROUP_DIM = 64
F_WIDTH = F_GROUPS * F_GROUP_DIM
GRID_W = 64
ROPE_AXIS_DIM = HEAD_DIM // 2
ROPE_HALF = ROPE_AXIS_DIM // 2
ROPE_THETA = 10000.0
N_EXPERT_GROUPS = 4
EXPERTS_PER_GROUP = 4
N_EXPERTS = N_EXPERT_GROUPS * EXPERTS_PER_GROUP
PAIRS_PER_GROUP = 6
N_BUCKETS = N_EXPERT_GROUPS * PAIRS_PER_GROUP
N_MOD = 6
EPS = 1e-6
ATTN_SCALE = HEAD_DIM ** -0.5

LANES = 128
SUBLANES = 8

TOKEN_TILE = 256
MOE_TILE = 256
ATTN_Q_TILE = 256
ATTN_K_TILE = 512
RANK_CHUNK = 512
BUCKET_ROWS = 32
VMEM_LIMIT = 48 * 1024 * 1024
NEG_BIG = -1e30


def _split(x):
    hi = x.astype(BF16)
    lo = (x - hi.astype(F32)).astype(BF16)
    return hi, lo


def _mm(a, b):
    return jnp.dot(a, b, preferred_element_type=F32)


def _mm3(a, b):
    ah, al = _split(a)
    bh, bl = _split(b)
    return _mm(ah, bh) + (_mm(ah, bl) + _mm(al, bh))


def _silu(x):
    return x / (1.0 + jnp.exp(-x))


def _params(*sem):
    return pltpu.CompilerParams(dimension_semantics=sem, vmem_limit_bytes=VMEM_LIMIT)


def _mod_kernel(c_ref, w_ref, b_ref, o_ref):
    o_ref[0] = _mm3(_silu(c_ref[...]), w_ref[0]) + b_ref[0]


def _adaln_mod(cs, w_mod, b_mod):
    n_layers, d, m = w_mod.shape
    r = cs.shape[0]
    tn = d
    return pl.pallas_call(
        _mod_kernel,
        grid=(n_layers, m // tn),
        in_specs=[
            pl.BlockSpec((r, d), lambda l, j: (0, 0)),
            pl.BlockSpec((1, d, tn), lambda l, j: (l, 0, j)),
            pl.BlockSpec((1, 1, tn), lambda l, j: (l, 0, j)),
        ],
        out_specs=pl.BlockSpec((1, r, tn), lambda l, j: (l, 0, j)),
        out_shape=jax.ShapeDtypeStruct((n_layers, r, m), F32),
        compiler_params=_params("arbitrary", "arbitrary"),
        name="adaln_mod",
    )(cs, w_mod, b_mod.reshape(n_layers, 1, m))


def _head_norm(z, gain, bd):
    outs = []
    for j in range(z.shape[1] // LANES):
        zj = z[:, LANES * j:LANES * (j + 1)]
        hi, lo = _split(zj * zj)
        msq = _mm(hi, bd) + _mm(lo, bd)
        outs.append(zj * lax.rsqrt(msq + EPS))
    return jnp.concatenate(outs, axis=1) * gain


def _rope(z, cos, sin):
    lane = lax.broadcasted_iota(jnp.int32, (z.shape[0], LANES), 1)
    first_half = (lane % ROPE_AXIS_DIM) < ROPE_HALF
    outs = []
    for j in range(z.shape[1] // LANES):
        zj = z[:, LANES * j:LANES * (j + 1)]
        partner = jnp.where(first_half, pltpu.roll(zj, LANES - ROPE_HALF, 1), pltpu.roll(zj, ROPE_HALF, 1))
        outs.append(zj * cos + partner * sin)
    return jnp.concatenate(outs, axis=1)


def _inproj_kernel(*refs, latent):
    if latent:
        (x_ref, mod_ref, n1_ref, w_ref, qg_ref, kg_ref, bd_ref, cos_ref, sin_ref,
         f_ref, q_ref, kt_ref, vx_ref) = refs
    else:
        (x_ref, mod_ref, n1_ref, w_ref, qg_ref, kg_ref, bd_ref,
         f_ref, q_ref, kt_ref, vx_ref, kc_ref, vc_ref) = refs
    x = x_ref[...]
    tt = x.shape[0]
    h = x * lax.rsqrt(jnp.mean(x * x, axis=-1, keepdims=True) + EPS) * n1_ref[...]
    h = h * (1.0 + mod_ref[0, 1:2, :]) + mod_ref[0, 0:1, :]
    p = _mm(h.astype(BF16), w_ref[...])
    f_ref[...] = p[:, :F_WIDTH]
    bd = bd_ref[...]
    q = _head_norm(p[:, F_WIDTH:F_WIDTH + Q_WIDTH], qg_ref[...], bd)
    k = _head_norm(p[:, F_WIDTH + Q_WIDTH:F_WIDTH + Q_WIDTH + KV_WIDTH], kg_ref[...], bd)
    v = p[:, F_WIDTH + Q_WIDTH + KV_WIDTH:]
    if latent:
        cos = cos_ref[...]
        sin = sin_ref[...]
        q = _rope(q, cos, sin)
        k = _rope(k, cos, sin)
    else:
        kc_ref[...] = k
        vc_ref[...] = v
    q_ref[...] = (q * ATTN_SCALE).astype(BF16)
    kt_ref[0] = k.T.reshape(N_KV_HEADS, HEAD_DIM, tt).astype(BF16)
    lane = lax.broadcasted_iota(jnp.int32, (tt, LANES), 1)
    ones_col = (lane == HEAD_DIM).astype(F32)
    for j in range(KV_WIDTH // LANES):
        vj = v[:, LANES * j:LANES * (j + 1)]
        vx_ref[0, 2 * j] = jnp.where(lane < HEAD_DIM, vj, ones_col).astype(BF16)
        vx_ref[0, 2 * j + 1] = jnp.where(lane < HEAD_DIM, pltpu.roll(vj, HEAD_DIM, 1), ones_col).astype(BF16)


def _inproj(x, modl, n1, w_in, qg, kg, bd, rope_tabs, batch, seq, latent):
    d = x.shape[1]
    tt = TOKEN_TILE
    nt = seq // tt
    in_w = w_in.shape[1]
    row = lambda b, i: (b * nt + i, 0)
    const = lambda b, i: (0, 0)
    mod_map = (lambda b, i: (b + 1, 0, 0)) if latent else (lambda b, i: (0, 0, 0))
    in_specs = [
        pl.BlockSpec((tt, d), row),
        pl.BlockSpec((1, N_MOD, d), mod_map),
        pl.BlockSpec((1, d), const),
        pl.BlockSpec((d, in_w), const),
        pl.BlockSpec((1, Q_WIDTH), const),
        pl.BlockSpec((1, KV_WIDTH), const),
        pl.BlockSpec((LANES, LANES), const),
    ]
    args = [x, modl, n1, w_in, qg, kg, bd]
    if latent:
        in_specs += [pl.BlockSpec((tt, LANES), lambda b, i: (i, 0))] * 2
        args += list(rope_tabs)
    t = batch * seq
    out_shape = [
        jax.ShapeDtypeStruct((t, F_WIDTH), F32),
        jax.ShapeDtypeStruct((t, Q_WIDTH), BF16),
        jax.ShapeDtypeStruct((batch, N_KV_HEADS, HEAD_DIM, seq), BF16),
        jax.ShapeDtypeStruct((batch, N_KV_HEADS, seq, LANES), BF16),
    ]
    out_specs = [
        pl.BlockSpec((tt, F_WIDTH), row),
        pl.BlockSpec((tt, Q_WIDTH), row),
        pl.BlockSpec((1, N_KV_HEADS, HEAD_DIM, tt), lambda b, i: (b, 0, 0, i)),
        pl.BlockSpec((1, N_KV_HEADS, tt, LANES), lambda b, i: (b, 0, i, 0)),
    ]
    if not latent:
        out_shape += [jax.ShapeDtypeStruct((t, KV_WIDTH), F32)] * 2
        out_specs += [pl.BlockSpec((tt, KV_WIDTH), row)] * 2
    return pl.pallas_call(
        functools.partial(_inproj_kernel, latent=latent),
        grid=(batch, nt),
        in_specs=in_specs,
        out_specs=out_specs,
        out_shape=out_shape,
        compiler_params=_params("arbitrary", "arbitrary"),
        name="inproj_latent" if latent else "inproj_ctx",
    )(*args)


def _dft_cos_sin(n):
    idx = np.arange(n, dtype=np.int64)
    ang = 2.0 * np.pi * ((idx[:, None] * idx[None, :]) % n).astype(np.float64) / n
    return np.cos(ang), np.sin(ang)


def _block_diag(m, reps):
    n = m.shape[0]
    out = np.zeros((n * reps, n * reps), m.dtype)
    for g in range(reps):
        out[g * n:(g + 1) * n, g * n:(g + 1) * n] = m
    return out


def _channel_dft():
    c, s = _dft_cos_sin(F_GROUP_DIM)
    return (jnp.asarray(_block_diag(c, F_GROUPS), F32), jnp.asarray(_block_diag(s, F_GROUPS), F32))


def _fourier_dense_kernel(z_ref, cn_ref, sn_ref, bdc_ref, bds_ref, wf_ref, o_ref, *, scale):
    z = z_ref[...]
    zc = _mm3(z, bdc_ref[...])
    zs = _mm3(z, bds_ref[...])
    y = (_mm3(cn_ref[...], zc) - _mm3(sn_ref[...], zs)) * scale
    o_ref[...] = _mm3(y, wf_ref[...])


def _fourier_dense(f_in, w_f, batch, seq):
    cn, sn = _dft_cos_sin(seq)
    bdc, bds = _channel_dft()
    const = lambda b: (0, 0)
    return pl.pallas_call(
        functools.partial(_fourier_dense_kernel, scale=float((seq * F_GROUP_DIM) ** -0.5)),
        grid=(batch,),
        in_specs=[
            pl.BlockSpec((seq, F_WIDTH), lambda b: (b, 0)),
            pl.BlockSpec((seq, seq), const),
            pl.BlockSpec((seq, seq), const),
            pl.BlockSpec((F_WIDTH, F_WIDTH), const),
            pl.BlockSpec((F_WIDTH, F_WIDTH), const),
            pl.BlockSpec((F_WIDTH, F_WIDTH), const),
        ],
        out_specs=pl.BlockSpec((seq, F_WIDTH), lambda b: (b, 0)),
        out_shape=jax.ShapeDtypeStruct((batch * seq, F_WIDTH), F32),
        compiler_params=_params("arbitrary"),
        name="fourier_dense",
    )(f_in, jnp.asarray(cn, F32), jnp.asarray(sn, F32), bdc, bds, w_f)


def _fft_rows_kernel(z_ref, cs_ref, y_ref):
    y_ref[0] = _mm3(cs_ref[...], z_ref[0])


def _fft_cols_kernel(y_ref, g_ref, bdc_ref, bds_ref, wf_ref, o_ref, *, scale, cb):
    half = GRID_W
    for cc in range(cb):
        ystack = jnp.concatenate([y_ref[0, 0, cc], y_ref[0, 1, cc]], axis=0)
        xc = _mm3(g_ref[cc], ystack)
        o = (_mm3(xc[:half], bdc_ref[...]) + _mm3(xc[half:], bds_ref[...])) * scale
        o_ref[0, :, F_WIDTH * cc:F_WIDTH * (cc + 1)] = _mm3(o, wf_ref[...])


def _fourier_fft(f_in, w_f, batch, seq):
    r = seq // GRID_W
    wide = GRID_W * F_WIDTH
    cr, sr = _dft_cos_sin(r)
    cs = jnp.asarray(np.concatenate([cr, -sr], axis=0), F32)
    bb = np.arange(GRID_W, dtype=np.int64)
    dd = np.arange(GRID_W, dtype=np.int64)
    cc = np.arange(r, dtype=np.int64)
    ang = 2.0 * np.pi * ((bb[None, None, :] * (r * dd[None, :, None] + cc[:, None, None])) % seq) / seq
    gr, gi = np.cos(ang), -np.sin(ang)
    g = jnp.asarray(np.concatenate([np.concatenate([gr, -gi], axis=2),
                                    np.concatenate([gi, gr], axis=2)], axis=1), F32)
    bdc, bds = _channel_dft()

    ch = min(wide, 2048)
    y = pl.pallas_call(
        _fft_rows_kernel,
        grid=(batch, wide // ch),
        in_specs=[pl.BlockSpec((1, r, ch), lambda b, j: (b, 0, j)),
                  pl.BlockSpec((2 * r, r), lambda b, j: (0, 0))],
        out_specs=pl.BlockSpec((1, 2 * r, ch), lambda b, j: (b, 0, j)),
        out_shape=jax.ShapeDtypeStruct((batch, 2 * r, wide), F32),
        compiler_params=_params("arbitrary", "arbitrary"),
        name="fft_rows",
    )(f_in.reshape(batch, r, wide), cs)

    cb = min(r, 8)
    const = lambda b, j: (0, 0)
    out = pl.pallas_call(
        functools.partial(_fft_cols_kernel, scale=float((seq * F_GROUP_DIM) ** -0.5), cb=cb),
        grid=(batch, r // cb),
        in_specs=[
            pl.BlockSpec((1, 2, cb, GRID_W, F_WIDTH), lambda b, j: (b, 0, j, 0, 0)),
            pl.BlockSpec((cb, 2 * GRID_W, 2 * GRID_W), lambda b, j: (j, 0, 0)),
            pl.BlockSpec((F_WIDTH, F_WIDTH), const),
            pl.BlockSpec((F_WIDTH, F_WIDTH), const),
            pl.BlockSpec((F_WIDTH, F_WIDTH), const),
        ],
        out_specs=pl.BlockSpec((1, GRID_W, cb * F_WIDTH), lambda b, j: (b, 0, j)),
        out_shape=jax.ShapeDtypeStruct((batch, GRID_W, r * F_WIDTH), F32),
        compiler_params=_params("arbitrary", "arbitrary"),
        name="fft_cols",
    )(y.reshape(batch, 2, r, GRID_W, F_WIDTH), g, bdc, bds, w_f)
    return out.reshape(batch * seq, F_WIDTH)


def _attn_kernel(*refs, tk, n_chunks, has_ctx):
    if has_ctx:
        q_ref, kt_ref, vx_ref, ckt_ref, cvx_ref, o_ref = refs
    else:
        q_ref, kt_ref, vx_ref, o_ref = refs
    tq = q_ref.shape[0]
    q = q_ref[...]
    m_rows = KV_GROUP * tq
    outs = []
    for g in range(N_KV_HEADS):
        heads = [q[:, HEAD_DIM * (KV_GROUP * g + h):HEAD_DIM * (KV_GROUP * g + h + 1)] for h in range(KV_GROUP)]
        qg = jnp.concatenate(heads, axis=0)

        def step(kt, vv, carry, qg=qg):
            m, acc = carry
            s = _mm(qg, kt)
            m_new = jnp.maximum(m, jnp.max(s, axis=-1, keepdims=True))
            p = jnp.exp(s - m_new).astype(BF16)
            return m_new, acc * jnp.exp(m - m_new) + _mm(p, vv)

        carry = (jnp.full((m_rows, 1), NEG_BIG, F32), jnp.zeros((m_rows, LANES), F32))
        if has_ctx:
            carry = step(ckt_ref[0, g], cvx_ref[0, g], carry)

        def body(i, carry, g=g, step=step):
            off = pl.multiple_of(i * tk, tk)
            return step(kt_ref[0, g, :, pl.ds(off, tk)], vx_ref[0, g, pl.ds(off, tk), :], carry)

        _, acc = lax.fori_loop(0, n_chunks, body, carry)
        o = acc[:, :HEAD_DIM] / acc[:, HEAD_DIM:HEAD_DIM + 1]
        outs += [o[h * tq:(h + 1) * tq] for h in range(KV_GROUP)]
    o_ref[...] = jnp.concatenate(outs, axis=1).astype(BF16)


def _attention(q, kt, vx, ctx, batch, seq):
    tq = min(ATTN_Q_TILE, seq)
    tk = min(ATTN_K_TILE, seq)
    nq = seq // tq
    in_specs = [
        pl.BlockSpec((tq, Q_WIDTH), lambda b, i: (b * nq + i, 0)),
        pl.BlockSpec((1, N_KV_HEADS, HEAD_DIM, seq), lambda b, i: (b, 0, 0, 0)),
        pl.BlockSpec((1, N_KV_HEADS, seq, LANES), lambda b, i: (b, 0, 0, 0)),
    ]
    args = [q, kt, vx]
    if ctx is not None:
        ckt, cvx = ctx
        past = ckt.shape[-1]
        in_specs += [
            pl.BlockSpec((1, N_KV_HEADS, HEAD_DIM, past), lambda b, i: (b, 0, 0, 0)),
            pl.BlockSpec((1, N_KV_HEADS, past, LANES), lambda b, i: (b, 0, 0, 0)),
        ]
        args += [ckt, cvx]
    return pl.pallas_call(
        functools.partial(_attn_kernel, tk=tk, n_chunks=seq // tk, has_ctx=ctx is not None),
        grid=(batch, nq),
        in_specs=in_specs,
        out_specs=pl.BlockSpec((tq, Q_WIDTH), lambda b, i: (b * nq + i, 0)),
        out_shape=jax.ShapeDtypeStruct((batch * seq, Q_WIDTH), BF16),
        compiler_params=_params("arbitrary", "arbitrary"),
        name="attention_latent" if ctx is not None else "attention_ctx",
    )(*args)


def _first_index(vals, target):
    idx = jnp.full(target.shape, float(len(vals) - 1), F32)
    for j in range(len(vals) - 2, -1, -1):
        idx = jnp.where(vals[j] == target, float(j), idx)
    return idx


def _route(lt):
    rows = [lt[i:i + 1, :] for i in range(N_EXPERT_GROUPS + N_EXPERTS)]
    gl = rows[:N_EXPERT_GROUPS]
    gmax = functools.reduce(jnp.maximum, gl)
    gidx = _first_index(gl, gmax)
    g_w = 1.0 / functools.reduce(lambda a, b: a + b, [jnp.exp(v - gmax) for v in gl])
    es = []
    for j in range(EXPERTS_PER_GROUP):
        sel = rows[N_EXPERT_GROUPS + (N_EXPERT_GROUPS - 1) * EXPERTS_PER_GROUP + j]
        for g in range(N_EXPERT_GROUPS - 2, -1, -1):
            sel = jnp.where(gidx == float(g), rows[N_EXPERT_GROUPS + g * EXPERTS_PER_GROUP + j], sel)
        es.append(sel)
    e1 = functools.reduce(jnp.maximum, es)
    i1 = _first_index(es, e1)
    rest = [jnp.where(i1 == float(j), -jnp.inf, es[j]) for j in range(EXPERTS_PER_GROUP)]
    e2 = functools.reduce(jnp.maximum, rest)
    i2 = _first_index(rest, e2)
    t = jnp.exp(e2 - e1)
    w1 = g_w / (1.0 + t)
    w2 = w1 * t
    lo = jnp.minimum(i1, i2)
    hi = jnp.maximum(i1, i2)
    w_lo = jnp.where(i1 < i2, w1, w2)
    w_hi = jnp.where(i1 < i2, w2, w1)
    pair = lo * (7.0 - lo) * 0.5 + (hi - lo - 1.0)
    return gidx * float(PAIRS_PER_GROUP) + pair, w_lo, w_hi


def _outproj_kernel(x_ref, f_ref, a_ref, mod_ref, n2_ref, wo_ref, wr_ref, br_ref, x1_ref, h2_ref, info_ref):
    d = x_ref.shape[1]
    tt = x_ref.shape[0]
    mix = _mm(f_ref[...].astype(BF16), wo_ref[:F_WIDTH, :]) + _mm(a_ref[...], wo_ref[F_WIDTH:, :])
    x1 = x_ref[...] + mod_ref[0, 2:3, :] * mix
    x1_ref[...] = x1
    h2 = x1 * lax.rsqrt(jnp.mean(x1 * x1, axis=-1, keepdims=True) + EPS) * n2_ref[...]
    h2 = h2 * (1.0 + mod_ref[0, 4:5, :]) + mod_ref[0, 3:4, :]
    logits = _mm3(h2, wr_ref[...]) + br_ref[...]
    bucket, w_lo, w_hi = _route(logits.T)
    h2_ref[:, :d] = h2
    row = lax.broadcasted_iota(jnp.int32, (LANES, tt), 0)
    wts = jnp.where(row == 0, w_lo, jnp.where(row == 1, w_hi, 0.0))
    h2_ref[:, d:] = wts.T
    row8 = lax.broadcasted_iota(jnp.int32, (SUBLANES, tt), 0)
    info_ref[...] = jnp.where(row8 == 0, bucket, 0.0)


def _outproj(x, f_out, a_out, modl, n2, w_o, wr, br, batch, seq, latent):
    d = x.shape[1]
    tt = TOKEN_TILE
    nt = seq // tt
    t = batch * seq
    row = lambda b, i: (b * nt + i, 0)
    const = lambda b, i: (0, 0)
    mod_map = (lambda b, i: (b + 1, 0, 0)) if latent else (lambda b, i: (0, 0, 0))
    return pl.pallas_call(
        _outproj_kernel,
        grid=(batch, nt),
        in_specs=[
            pl.BlockSpec((tt, d), row),
            pl.BlockSpec((tt, F_WIDTH), row),
            pl.BlockSpec((tt, Q_WIDTH), row),
            pl.BlockSpec((1, N_MOD, d), mod_map),
            pl.BlockSpec((1, d), const),
            pl.BlockSpec((F_WIDTH + Q_WIDTH, d), const),
            pl.BlockSpec((d, LANES), const),
            pl.BlockSpec((1, LANES), const),
        ],
        out_specs=[
            pl.BlockSpec((tt, d), row),
            pl.BlockSpec((tt, d + LANES), row),
            pl.BlockSpec((SUBLANES, tt), lambda b, i: (0, b * nt + i)),
        ],
        out_shape=[
            jax.ShapeDtypeStruct((t, d), F32),
            jax.ShapeDtypeStruct((t, d + LANES), F32),
            jax.ShapeDtypeStruct((SUBLANES, t), F32),
        ],
        compiler_params=_params("arbitrary", "arbitrary"),
        name="outproj_latent" if latent else "outproj_ctx",
    )(x, f_out, a_out, modl, n2, w_o, wr, br)


def _rank_kernel(b_ref, pos_ref, start_ref, *, n_chunks, ch):
    rows = lax.broadcasted_iota(jnp.int32, (BUCKET_ROWS, ch), 0).astype(F32)
    ones = jnp.ones((ch, LANES), BF16)
    upper = (lax.broadcasted_iota(jnp.int32, (ch, ch), 0) < lax.broadcasted_iota(jnp.int32, (ch, ch), 1)).astype(BF16)

    def onehot(i):
        off = pl.multiple_of(i * ch, ch)
        return (rows == b_ref[:, pl.ds(off, ch)]).astype(BF16)

    count = lax.fori_loop(0, n_chunks, lambda i, c: c + _mm(onehot(i), ones), jnp.zeros((BUCKET_ROWS, LANES), F32))
    n_tiles = jnp.floor((count + float(MOE_TILE - 1)) * (1.0 / MOE_TILE))
    lower = (lax.broadcasted_iota(jnp.int32, (BUCKET_ROWS, BUCKET_ROWS), 1)
             < lax.broadcasted_iota(jnp.int32, (BUCKET_ROWS, BUCKET_ROWS), 0)).astype(BF16)
    start_tiles = _mm(lower, n_tiles.astype(BF16))
    start_ref[...] = start_tiles
    start_rows = start_tiles * float(MOE_TILE)

    def body(i, carry):
        oh = onehot(i)
        prefix = _mm(oh, upper)
        base = jnp.concatenate([start_rows + carry] * (ch // LANES), axis=1)
        pos = jnp.sum(oh.astype(F32) * (prefix + base), axis=0, keepdims=True)
        off = pl.multiple_of(i * ch, ch)
        pos_ref[:, pl.ds(off, ch)] = pos.astype(jnp.int32)
        return carry + _mm(oh, ones)

    lax.fori_loop(0, n_chunks, body, jnp.zeros((BUCKET_ROWS, LANES), F32))


def _rank(bucket_row):
    t = bucket_row.shape[1]
    ch = RANK_CHUNK
    return pl.pallas_call(
        functools.partial(_rank_kernel, n_chunks=t // ch, ch=ch),
        out_shape=[jax.ShapeDtypeStruct((1, t), jnp.int32), jax.ShapeDtypeStruct((BUCKET_ROWS, LANES), F32)],
        compiler_params=pltpu.CompilerParams(vmem_limit_bytes=VMEM_LIMIT),
        name="bucket_rank",
    )(bucket_row)


def _dispatch_kernel(pos_ref, h_ref, dst_in_ref, dst_ref, sem):
    del dst_in_ref
    tt = h_ref.shape[0]

    def body(r, c):
        pltpu.make_async_copy(h_ref.at[pl.ds(r, 1)], dst_ref.at[pl.ds(pos_ref[0, 0, r], 1)], sem).start()
        return c

    lax.fori_loop(0, tt, body, 0)
    pltpu.make_async_copy(h_ref, dst_ref.at[pl.ds(0, tt)], sem).wait()


def _dispatch(pos3, h2e, dst):
    nt, _, tt = pos3.shape
    w = h2e.shape[1]
    return pl.pallas_call(
        _dispatch_kernel,
        grid=(nt,),
        in_specs=[
            pl.BlockSpec((1, 1, tt), lambda i: (i, 0, 0), memory_space=pltpu.SMEM),
            pl.BlockSpec((tt, w), lambda i: (i, 0)),
            pl.BlockSpec(memory_space=pl.ANY),
        ],
        out_specs=pl.BlockSpec(memory_space=pl.ANY),
        out_shape=jax.ShapeDtypeStruct(dst.shape, dst.dtype),
        scratch_shapes=[pltpu.SemaphoreType.DMA(())],
        input_output_aliases={2: 0},
        compiler_params=_params("arbitrary"),
        name="moe_dispatch",
    )(pos3, h2e, dst)


def _moe_kernel(ea_ref, eb_ref, valid_ref, h_ref, wga, wua, wda, wgb, wub, wdb, o_ref):
    del ea_ref, eb_ref
    i = pl.program_id(0)
    d = o_ref.shape[1]

    @pl.when(valid_ref[i] == 1)
    def _():
        hb = h_ref[:, :d].astype(BF16)

        def expert(wg, wu, wd):
            hid = _silu(_mm(hb, wg[0])) * _mm(hb, wu[0])
            return _mm(hid.astype(BF16), wd[0])

        ya = expert(wga, wua, wda)
        yb = expert(wgb, wub, wdb)
        o_ref[...] = h_ref[:, d:d + 1] * ya + h_ref[:, d + 1:d + 2] * yb

    @pl.when(valid_ref[i] == 0)
    def _():
        o_ref[...] = jnp.zeros(o_ref.shape, F32)


def _moe(ea, eb, valid, h2s, wg, wu, wd):
    n_tiles = ea.shape[0]
    tm = MOE_TILE
    d = wd.shape[2]
    de = wd.shape[1]
    a_map = lambda i, ea, eb, valid: (ea[i], 0, 0)
    b_map = lambda i, ea, eb, valid: (eb[i], 0, 0)
    row = lambda i, ea, eb, valid: (i, 0)
    grid_spec = pltpu.PrefetchScalarGridSpec(
        num_scalar_prefetch=3,
        grid=(n_tiles,),
        in_specs=[
            pl.BlockSpec((tm, h2s.shape[1]), row),
            pl.BlockSpec((1, d, de), a_map), pl.BlockSpec((1, d, de), a_map), pl.BlockSpec((1, de, d), a_map),
            pl.BlockSpec((1, d, de), b_map), pl.BlockSpec((1, d, de), b_map), pl.BlockSpec((1, de, d), b_map),
        ],
        out_specs=pl.BlockSpec((tm, d), row),
    )
    return pl.pallas_call(
        _moe_kernel,
        grid_spec=grid_spec,
        out_shape=jax.ShapeDtypeStruct((n_tiles * tm, d), F32),
        compiler_params=_params("arbitrary"),
        name="moe_experts",
    )(ea, eb, valid, h2s, wg, wu, wd, wg, wu, wd)


def _combine_kernel(pos_ref, x_ref, mod_ref, ys_ref, o_ref, buf, sem):
    tt = x_ref.shape[0]

    def body(r, c):
        pltpu.make_async_copy(ys_ref.at[pl.ds(pos_ref[0, 0, r], 1)], buf.at[pl.ds(r, 1)], sem).start()
        return c

    lax.fori_loop(0, tt, body, 0)
    pltpu.make_async_copy(ys_ref.at[pl.ds(0, tt)], buf, sem).wait()
    o_ref[...] = x_ref[...] + mod_ref[0, 5:6, :] * buf[...]


def _combine(pos3, x1, modl, ys, batch, seq, latent):
    d = x1.shape[1]
    tt = TOKEN_TILE
    nt = seq // tt
    mod_map = (lambda i: (i // nt + 1, 0, 0)) if latent else (lambda i: (0, 0, 0))
    return pl.pallas_call(
        _combine_kernel,
        grid=(batch * nt,),
        in_specs=[
            pl.BlockSpec((1, 1, tt), lambda i: (i, 0, 0), memory_space=pltpu.SMEM),
            pl.BlockSpec((tt, d), lambda i: (i, 0)),
            pl.BlockSpec((1, N_MOD, d), mod_map),
            pl.BlockSpec(memory_space=pl.ANY),
        ],
        out_specs=pl.BlockSpec((tt, d), lambda i: (i, 0)),
        out_shape=jax.ShapeDtypeStruct(x1.shape, F32),
        scratch_shapes=[pltpu.VMEM((tt, d), F32), pltpu.SemaphoreType.DMA(())],
        compiler_params=_params("arbitrary"),
        name="moe_combine_latent" if latent else "moe_combine_ctx",
    )(pos3, x1, modl, ys)


def _rope_tables(seq):
    rows = seq // GRID_W
    row = jnp.repeat(jnp.arange(rows, dtype=F32), GRID_W)
    col = jnp.tile(jnp.arange(GRID_W, dtype=F32), rows)
    inv = ROPE_THETA ** (-jnp.arange(0, ROPE_AXIS_DIM, 2, dtype=F32) / ROPE_AXIS_DIM)
    cos_parts, sin_parts = [], []
    for pos in (row, col):
        ang = pos[:, None] * inv
        cos_parts += [jnp.cos(ang), jnp.cos(ang)]
        sin_parts += [-jnp.sin(ang), jnp.sin(ang)]
    cos = jnp.concatenate(cos_parts, axis=-1)
    sin = jnp.concatenate(sin_parts, axis=-1)
    return jnp.tile(cos, (1, LANES // HEAD_DIM)), jnp.tile(sin, (1, LANES // HEAD_DIM))


def _tile_tables(start_tiles, n_tiles):
    st = start_tiles[:N_BUCKETS + 1, 0].astype(jnp.int32)
    tiles = jnp.arange(n_tiles, dtype=jnp.int32)
    total = st[N_BUCKETS]
    bucket = jnp.sum((jnp.minimum(tiles, total - 1)[:, None] >= st[None, 1:]).astype(jnp.int32), axis=1)
    bucket = jnp.minimum(bucket, N_BUCKETS - 1)
    pairs = [(a, b) for a in range(EXPERTS_PER_GROUP) for b in range(a + 1, EXPERTS_PER_GROUP)]
    lo = jnp.asarray([p[0] for p in pairs], jnp.int32)
    hi = jnp.asarray([p[1] for p in pairs], jnp.int32)
    grp = bucket // PAIRS_PER_GROUP
    pair = bucket % PAIRS_PER_GROUP
    ea = grp * EXPERTS_PER_GROUP + lo[pair]
    eb = grp * EXPERTS_PER_GROUP + hi[pair]
    return ea, eb, (tiles < total).astype(jnp.int32)


def _ctx_kv(cache_k_l, cache_v_l):
    b, p = cache_k_l.shape[:2]
    ckt = cache_k_l.transpose(0, 2, 3, 1).astype(BF16)
    v = cache_v_l.transpose(0, 2, 1, 3)
    pad = jnp.zeros((b, N_KV_HEADS, p, LANES - HEAD_DIM), F32).at[..., 0].set(1.0)
    return ckt, jnp.concatenate([v, pad], axis=-1).astype(BF16)


def kernel(x_prompt, x_sample, cache_k, cache_v, c, c_ctx, norm1_g, norm2_g, w_mod, b_mod, w_in, w_fourier,
           q_norm_g, k_norm_g, w_out, w_router_group, b_router_group, w_router_expert, b_router_expert,
           w_gate, w_up, w_down):
    batch, seq, d = x_prompt.shape
    dec_batch, dec_seq, _ = x_sample.shape
    depth = w_in.shape[0]
    streams = [
        dict(latent=False, batch=batch, seq=seq, x=x_prompt.reshape(batch * seq, d)),
        dict(latent=True, batch=dec_batch, seq=dec_seq, x=x_sample.reshape(dec_batch * dec_seq, d)),
    ]
    t_total = sum(s["batch"] * s["seq"] for s in streams)
    n_sorted_tiles = t_total // MOE_TILE + N_BUCKETS

    cond_rows = 2 * SUBLANES
    cs = jnp.zeros((cond_rows, d), F32).at[0].set(c_ctx).at[1:1 + dec_batch].set(c)
    mod = _adaln_mod(cs, w_mod, b_mod).reshape(depth, cond_rows, N_MOD, d)

    seg = np.arange(LANES) // HEAD_DIM
    bd = jnp.asarray((seg[:, None] == seg[None, :]).astype(np.float32) / HEAD_DIM, BF16)
    rope_tabs = _rope_tables(dec_seq)
    kc_layers, vc_layers = [], []

    for l in range(depth):
        modl = mod[l]
        w_in_l = w_in[l].astype(BF16)
        w_o_l = w_out[l].astype(BF16)
        qg = jnp.tile(q_norm_g[l], N_Q_HEADS)[None, :]
        kg = jnp.tile(k_norm_g[l], N_KV_HEADS)[None, :]
        n_router = N_EXPERT_GROUPS + N_EXPERTS
        wr = jnp.concatenate([w_router_group[l], w_router_expert[l], jnp.zeros((d, LANES - n_router), F32)], axis=1)
        br = jnp.concatenate([b_router_group[l], b_router_expert[l], jnp.zeros((LANES - n_router,), F32)])[None, :]
        wg_l, wu_l, wd_l = w_gate[l].astype(BF16), w_up[l].astype(BF16), w_down[l].astype(BF16)

        for s in streams:
            b_, n_, lat = s["batch"], s["seq"], s["latent"]
            res = _inproj(s["x"], modl, norm1_g[l][None, :], w_in_l, qg, kg, bd, rope_tabs, b_, n_, lat)
            f_in, q, kt, vx = res[:4]
            if lat:
                ctx = _ctx_kv(cache_k[:, l], cache_v[:, l])
            else:
                ctx = None
                kc_layers.append(res[4])
                vc_layers.append(res[5])
            fourier = _fourier_fft if n_ > TOKEN_TILE else _fourier_dense
            f_out = fourier(f_in, w_fourier[l], b_, n_)
            a_out = _attention(q, kt, vx, ctx, b_, n_)
            s["x1"], s["h2e"], s["info"] = _outproj(s["x"], f_out, a_out, modl, norm2_g[l][None, :], w_o_l, wr, br,
                                                    b_, n_, lat)

        bucket_row = jnp.concatenate([s["info"][0:1] for s in streams], axis=1)
        pos, start_tiles = _rank(bucket_row)
        ea, eb, valid = _tile_tables(start_tiles, n_sorted_tiles)

        h2s = jnp.zeros((n_sorted_tiles * MOE_TILE, d + LANES), F32)
        off = 0
        for s in streams:
            t_s = s["batch"] * s["seq"]
            s["pos3"] = pos[0, off:off + t_s].reshape(t_s // TOKEN_TILE, 1, TOKEN_TILE)
            off += t_s
            h2s = _dispatch(s["pos3"], s["h2e"], h2s)
        ys = _moe(ea, eb, valid, h2s, wg_l, wu_l, wd_l)
        for s in streams:
            s["x"] = _combine(s["pos3"], s["x1"], modl, ys, s["batch"], s["seq"], s["latent"])

    y_prompt = streams[0]["x"].reshape(batch, seq, d)
    y_sample = streams[1]["x"].reshape(dec_batch, dec_seq, d)
    cache_shape = (batch, depth, seq, N_KV_HEADS, HEAD_DIM)
    new_k = jnp.stack([k.reshape(batch, seq, KV_WIDTH) for k in kc_layers], axis=1).reshape(cache_shape)
    new_v = jnp.stack([v.reshape(batch, seq, KV_WIDTH) for v in vc_layers], axis=1).reshape(cache_shape)
    return (y_prompt, y_sample, new_k, new_v)
```

```python
import functools

import numpy as np
import jax
import jax.numpy as jnp
from jax import lax
from jax.experimental import pallas as pl
from jax.experimental.pallas import tpu as pltpu

F32 = jnp.float32
BF16 = jnp.bfloat16

HEAD_DIM = 64
N_Q_HEADS = 12
N_KV_HEADS = 4
KV_GROUP = N_Q_HEADS // N_KV_HEADS
Q_WIDTH = N_Q_HEADS * HEAD_DIM
GROUP_WIDTH = KV_GROUP * HEAD_DIM
KV_WIDTH = N_KV_HEADS * HEAD_DIM
F_GROUPS = 4
F_GROUP_DIM = 64
F_WIDTH = F_GROUPS * F_GROUP_DIM
GRID_W = 64
ROPE_AXIS_DIM = HEAD_DIM // 2
ROPE_HALF = ROPE_AXIS_DIM // 2
ROPE_THETA = 10000.0
N_EXPERT_GROUPS = 4
EXPERTS_PER_GROUP = 4
N_EXPERTS = N_EXPERT_GROUPS * EXPERTS_PER_GROUP
PAIRS_PER_GROUP = 6
N_BUCKETS = N_EXPERT_GROUPS * PAIRS_PER_GROUP
N_MOD = 6
EPS = 1e-6
ATTN_SCALE = HEAD_DIM ** -0.5
LOG2_E = 1.4426950408889634

LANES = 128
SUBLANES = 8

TOKEN_TILE = 256
MOE_TILE = 256
ATTN_Q_TILE = 512
ATTN_K_TILE = 512
FFT_TOKEN_TILE = 512
RANK_CHUNK = 512
BUCKET_ROWS = 32
VMEM_LIMIT = 48 * 1024 * 1024
NEG_BIG = -1e30


def _split(x):
    hi = x.astype(BF16)
    lo = (x - hi.astype(F32)).astype(BF16)
    return hi, lo


def _mm(a, b):
    return jnp.dot(a, b, preferred_element_type=F32)


def _mm3(a, b):
    ah, al = _split(a)
    bh, bl = _split(b)
    return _mm(ah, bh) + (_mm(ah, bl) + _mm(al, bh))


def _silu(x):
    return x / (1.0 + jnp.exp(-x))


def _uses_fft(seq):
    return seq > TOKEN_TILE


def _params(*sem):
    return pltpu.CompilerParams(dimension_semantics=sem, vmem_limit_bytes=VMEM_LIMIT)


def _mod_kernel(c_ref, w_ref, b_ref, o_ref):
    o_ref[0] = _mm3(_silu(c_ref[...]), w_ref[0]) + b_ref[0]


def _adaln_mod(cs, w_mod, b_mod):
    n_layers, d, m = w_mod.shape
    r = cs.shape[0]
    tn = d
    return pl.pallas_call(
        _mod_kernel,
        grid=(n_layers, m // tn),
        in_specs=[
            pl.BlockSpec((r, d), lambda l, j: (0, 0)),
            pl.BlockSpec((1, d, tn), lambda l, j: (l, 0, j)),
            pl.BlockSpec((1, 1, tn), lambda l, j: (l, 0, j)),
        ],
        out_specs=pl.BlockSpec((1, r, tn), lambda l, j: (l, 0, j)),
        out_shape=jax.ShapeDtypeStruct((n_layers, r, m), F32),
        compiler_params=_params("arbitrary", "arbitrary"),
        name="adaln_mod",
    )(cs, w_mod, b_mod.reshape(n_layers, 1, m))


def _head_norm(z, gain, bd):
    outs = []
    for j in range(z.shape[1] // LANES):
        zj = z[:, LANES * j:LANES * (j + 1)]
        hi, lo = _split(zj * zj)
        msq = _mm(hi, bd) + _mm(lo, bd)
        outs.append(zj * lax.rsqrt(msq + EPS))
    return jnp.concatenate(outs, axis=1) * gain


def _rope(z, cos, sin):
    lane = lax.broadcasted_iota(jnp.int32, (z.shape[0], LANES), 1)
    first_half = (lane % ROPE_AXIS_DIM) < ROPE_HALF
    outs = []
    for j in range(z.shape[1] // LANES):
        zj = z[:, LANES * j:LANES * (j + 1)]
        partner = jnp.where(first_half, pltpu.roll(zj, LANES - ROPE_HALF, 1), pltpu.roll(zj, ROPE_HALF, 1))
        outs.append(zj * cos + partner * sin)
    return jnp.concatenate(outs, axis=1)


def _inproj_kernel(*refs, latent, grid_major):
    if latent:
        (x_ref, mod_ref, n1_ref, w_ref, qg_ref, kg_ref, bd_ref, cos_ref, sin_ref,
         f_ref, q_ref, kt_ref, vx_ref) = refs
    else:
        (x_ref, mod_ref, n1_ref, w_ref, qg_ref, kg_ref, bd_ref,
         f_ref, q_ref, kt_ref, vx_ref, kc_ref, vc_ref) = refs
    x = x_ref[...]
    tt = x.shape[0]
    h = x * lax.rsqrt(jnp.mean(x * x, axis=-1, keepdims=True) + EPS) * n1_ref[...]
    h = h * (1.0 + mod_ref[0, 1:2, :]) + mod_ref[0, 0:1, :]
    p = _mm(h.astype(BF16), w_ref[...])
    if grid_major:
        for a in range(tt // GRID_W):
            f_ref[0, :, a, :] = p[GRID_W * a:GRID_W * (a + 1), :F_WIDTH]
    else:
        f_ref[...] = p[:, :F_WIDTH]
    bd = bd_ref[...]
    q = _head_norm(p[:, F_WIDTH:F_WIDTH + Q_WIDTH], qg_ref[...], bd)
    k = _head_norm(p[:, F_WIDTH + Q_WIDTH:F_WIDTH + Q_WIDTH + KV_WIDTH], kg_ref[...], bd)
    v = p[:, F_WIDTH + Q_WIDTH + KV_WIDTH:]
    if latent:
        cos = cos_ref[...]
        sin = sin_ref[...]
        q = _rope(q, cos, sin)
        k = _rope(k, cos, sin)
    else:
        kc_ref[...] = k
        vc_ref[...] = v
    qs = (q * (ATTN_SCALE * LOG2_E)).astype(BF16)
    for g in range(N_KV_HEADS):
        q_ref[g] = qs[:, GROUP_WIDTH * g:GROUP_WIDTH * (g + 1)]
    kt_ref[0] = k.T.reshape(N_KV_HEADS, HEAD_DIM, tt).astype(BF16)
    lane = lax.broadcasted_iota(jnp.int32, (tt, LANES), 1)
    ones_col = (lane == HEAD_DIM).astype(F32)
    for j in range(KV_WIDTH // LANES):
        vj = v[:, LANES * j:LANES * (j + 1)]
        vx_ref[0, 2 * j] = jnp.where(lane < HEAD_DIM, vj, ones_col).astype(BF16)
        vx_ref[0, 2 * j + 1] = jnp.where(lane < HEAD_DIM, pltpu.roll(vj, HEAD_DIM, 1), ones_col).astype(BF16)


def _inproj(x, modl, n1, w_in, qg, kg, bd, rope_tabs, batch, seq, latent):
    d = x.shape[1]
    grid_major = _uses_fft(seq)
    tt = FFT_TOKEN_TILE if grid_major else TOKEN_TILE
    nt = seq // tt
    in_w = w_in.shape[1]
    row = lambda b, i: (b * nt + i, 0)
    const = lambda b, i: (0, 0)
    mod_map = (lambda b, i: (b + 1, 0, 0)) if latent else (lambda b, i: (0, 0, 0))
    in_specs = [
        pl.BlockSpec((tt, d), row),
        pl.BlockSpec((1, N_MOD, d), mod_map),
        pl.BlockSpec((1, d), const),
        pl.BlockSpec((d, in_w), const),
        pl.BlockSpec((1, Q_WIDTH), const),
        pl.BlockSpec((1, KV_WIDTH), const),
        pl.BlockSpec((LANES, LANES), const),
    ]
    args = [x, modl, n1, w_in, qg, kg, bd]
    if latent:
        in_specs += [pl.BlockSpec((tt, LANES), lambda b, i: (i, 0))] * 2
        args += list(rope_tabs)
    t = batch * seq
    out_shape = [
        jax.ShapeDtypeStruct((batch, GRID_W, seq // GRID_W, F_WIDTH) if grid_major else (t, F_WIDTH), F32),
        jax.ShapeDtypeStruct((N_KV_HEADS, t, GROUP_WIDTH), BF16),
        jax.ShapeDtypeStruct((batch, N_KV_HEADS, HEAD_DIM, seq), BF16),
        jax.ShapeDtypeStruct((batch, N_KV_HEADS, seq, LANES), BF16),
    ]
    out_specs = [
        (pl.BlockSpec((1, GRID_W, tt // GRID_W, F_WIDTH), lambda b, i: (b, 0, i, 0)) if grid_major
         else pl.BlockSpec((tt, F_WIDTH), row)),
        pl.BlockSpec((N_KV_HEADS, tt, GROUP_WIDTH), lambda b, i: (0, b * nt + i, 0)),
        pl.BlockSpec((1, N_KV_HEADS, HEAD_DIM, tt), lambda b, i: (b, 0, 0, i)),
        pl.BlockSpec((1, N_KV_HEADS, tt, LANES), lambda b, i: (b, 0, i, 0)),
    ]
    if not latent:
        out_shape += [jax.ShapeDtypeStruct((t, KV_WIDTH), F32)] * 2
        out_specs += [pl.BlockSpec((tt, KV_WIDTH), row)] * 2
    return pl.pallas_call(
        functools.partial(_inproj_kernel, latent=latent, grid_major=grid_major),
        grid=(batch, nt),
        in_specs=in_specs,
        out_specs=out_specs,
        out_shape=out_shape,
        compiler_params=_params("arbitrary", "arbitrary"),
        name="inproj_latent" if latent else "inproj_ctx",
    )(*args)


def _dft_cos_sin(n):
    idx = np.arange(n, dtype=np.int64)
    ang = 2.0 * np.pi * ((idx[:, None] * idx[None, :]) % n).astype(np.float64) / n
    return np.cos(ang), np.sin(ang)


def _block_diag(m, reps):
    n = m.shape[0]
    out = np.zeros((n * reps, n * reps), m.dtype)
    for g in range(reps):
        out[g * n:(g + 1) * n, g * n:(g + 1) * n] = m
    return out


def _mxu_const(x):
    return jnp.asarray(x, F32).astype(BF16)


def _channel_dft():
    c, s = _dft_cos_sin(F_GROUP_DIM)
    return _mxu_const(_block_diag(c, F_GROUPS)), _mxu_const(_block_diag(s, F_GROUPS))


def _fourier_dense_kernel(z_ref, cn_ref, sn_ref, bdc_ref, bds_ref, wf_ref, o_ref, *, scale):
    z = z_ref[...].astype(BF16)
    zc = _mm(z, bdc_ref[...]).astype(BF16)
    zs = _mm(z, bds_ref[...]).astype(BF16)
    y = (_mm(cn_ref[...], zc) - _mm(sn_ref[...], zs)) * scale
    o_ref[...] = _mm(y.astype(BF16), wf_ref[...])


def _fourier_dense(f_in, w_f, batch, seq):
    cn, sn = _dft_cos_sin(seq)
    bdc, bds = _channel_dft()
    const = lambda b: (0, 0)
    return pl.pallas_call(
        functools.partial(_fourier_dense_kernel, scale=float((seq * F_GROUP_DIM) ** -0.5)),
        grid=(batch,),
        in_specs=[
            pl.BlockSpec((seq, F_WIDTH), lambda b: (b, 0)),
            pl.BlockSpec((seq, seq), const),
            pl.BlockSpec((seq, seq), const),
            pl.BlockSpec((F_WIDTH, F_WIDTH), const),
            pl.BlockSpec((F_WIDTH, F_WIDTH), const),
            pl.BlockSpec((F_WIDTH, F_WIDTH), const),
        ],
        out_specs=pl.BlockSpec((seq, F_WIDTH), lambda b: (b, 0)),
        out_shape=jax.ShapeDtypeStruct((batch * seq, F_WIDTH), F32),
        compiler_params=_params("arbitrary"),
        name="fourier_dense",
    )(f_in, _mxu_const(cn), _mxu_const(sn), bdc, bds, w_f)


def _fft_kernel(z_ref, cs_ref, g_ref, bdc_ref, bds_ref, wf_ref, o_ref, y_sc, *, scale, r, cb):
    def rows(b, carry):
        yb = _mm(cs_ref[...], z_ref[0, b].astype(BF16))
        y_sc[0, :, b, :] = yb[:r]
        y_sc[1, :, b, :] = yb[r:]
        return carry

    lax.fori_loop(0, GRID_W, rows, 0, unroll=8)

    def cols(j, carry):
        xr, xi = [], []
        for cc in range(cb):
            c = j * cb + cc
            ystack = jnp.concatenate([y_sc[0, c], y_sc[1, c]], axis=0).astype(BF16)
            xc = _mm(g_ref[c], ystack)
            xr.append(xc[:GRID_W])
            xi.append(xc[GRID_W:])
        xr = jnp.concatenate(xr, axis=0).astype(BF16)
        xi = jnp.concatenate(xi, axis=0).astype(BF16)
        o = (_mm(xr, bdc_ref[...]) + _mm(xi, bds_ref[...])) * scale
        o = _mm(o.astype(BF16), wf_ref[...])
        for cc in range(cb):
            o_ref[0, :, j * cb + cc, :] = o[GRID_W * cc:GRID_W * (cc + 1)]
        return carry

    lax.fori_loop(0, r // cb, cols, 0, unroll=2)


def _fourier_fft(f_in, w_f, batch, seq):
    r = seq // GRID_W
    cr, sr = _dft_cos_sin(r)
    cs = _mxu_const(np.concatenate([cr, -sr], axis=0))
    bb = np.arange(GRID_W, dtype=np.int64)
    dd = np.arange(GRID_W, dtype=np.int64)
    cc = np.arange(r, dtype=np.int64)
    ang = 2.0 * np.pi * ((bb[None, None, :] * (r * dd[None, :, None] + cc[:, None, None])) % seq) / seq
    gr, gi = np.cos(ang), -np.sin(ang)
    g = _mxu_const(np.concatenate([np.concatenate([gr, -gi], axis=2),
                                   np.concatenate([gi, gr], axis=2)], axis=1))
    bdc, bds = _channel_dft()
    cb = min(r, 8)
    const = lambda b: (0, 0)
    out = pl.pallas_call(
        functools.partial(_fft_kernel, scale=float((seq * F_GROUP_DIM) ** -0.5), r=r, cb=cb),
        grid=(batch,),
        in_specs=[
            pl.BlockSpec((1, GRID_W, r, F_WIDTH), lambda b: (b, 0, 0, 0)),
            pl.BlockSpec((2 * r, r), const),
            pl.BlockSpec((r, 2 * GRID_W, 2 * GRID_W), lambda b: (0, 0, 0)),
            pl.BlockSpec((F_WIDTH, F_WIDTH), const),
            pl.BlockSpec((F_WIDTH, F_WIDTH), const),
            pl.BlockSpec((F_WIDTH, F_WIDTH), const),
        ],
        out_specs=pl.BlockSpec((1, GRID_W, r, F_WIDTH), lambda b: (b, 0, 0, 0)),
        out_shape=jax.ShapeDtypeStruct((batch, GRID_W, r, F_WIDTH), F32),
        scratch_shapes=[pltpu.VMEM((2, r, GRID_W, F_WIDTH), F32)],
        compiler_params=_params("arbitrary"),
        name="fourier_fft",
    )(f_in, cs, g, bdc, bds, w_f)
    return out.reshape(batch * seq, F_WIDTH)


def _attn_kernel(*refs, tk, n_chunks, has_ctx):
    if has_ctx:
        q_ref, kt_ref, vx_ref, ckt_ref, cvx_ref, o_ref = refs
    else:
        q_ref, kt_ref, vx_ref, o_ref = refs
    tq = q_ref.shape[1]
    q = q_ref[0]
    qg = jnp.concatenate([q[:, HEAD_DIM * h:HEAD_DIM * (h + 1)] for h in range(KV_GROUP)], axis=0)
    m_rows = KV_GROUP * tq

    def step(kt, vv, carry):
        m, acc = carry
        s = _mm(qg, kt)
        tile_max = functools.reduce(
            jnp.maximum, [s[:, LANES * j:LANES * (j + 1)] for j in range(s.shape[1] // LANES)])
        m_new = jnp.maximum(m, jnp.max(tile_max, axis=-1, keepdims=True))
        p = jnp.exp2(s - m_new).astype(BF16)
        return m_new, acc * jnp.exp2(m - m_new) + _mm(p, vv)

    carry = (jnp.full((m_rows, 1), NEG_BIG, F32), jnp.zeros((m_rows, LANES), F32))
    if has_ctx:
        carry = step(ckt_ref[0, 0], cvx_ref[0, 0], carry)
    for i in range(n_chunks):
        carry = step(kt_ref[0, 0, :, tk * i:tk * (i + 1)], vx_ref[0, 0, tk * i:tk * (i + 1), :], carry)
    acc = carry[1]
    o = acc[:, :HEAD_DIM] / acc[:, HEAD_DIM:HEAD_DIM + 1]
    o_ref[0] = jnp.concatenate([o[h * tq:(h + 1) * tq] for h in range(KV_GROUP)], axis=1).astype(BF16)


def _attention(q, kt, vx, ctx, batch, seq):
    tq = min(ATTN_Q_TILE, seq)
    tk = min(ATTN_K_TILE, seq)
    nq = seq // tq
    q_spec = pl.BlockSpec((1, tq, GROUP_WIDTH), lambda b, g, i: (g, b * nq + i, 0))
    in_specs = [
        q_spec,
        pl.BlockSpec((1, 1, HEAD_DIM, seq), lambda b, g, i: (b, g, 0, 0)),
        pl.BlockSpec((1, 1, seq, LANES), lambda b, g, i: (b, g, 0, 0)),
    ]
    args = [q, kt, vx]
    if ctx is not None:
        ckt, cvx = ctx
        past = ckt.shape[-1]
        in_specs += [
            pl.BlockSpec((1, 1, HEAD_DIM, past), lambda b, g, i: (b, g, 0, 0)),
            pl.BlockSpec((1, 1, past, LANES), lambda b, g, i: (b, g, 0, 0)),
        ]
        args += [ckt, cvx]
    return pl.pallas_call(
        functools.partial(_attn_kernel, tk=tk, n_chunks=seq // tk, has_ctx=ctx is not None),
        grid=(batch, N_KV_HEADS, nq),
        in_specs=in_specs,
        out_specs=q_spec,
        out_shape=jax.ShapeDtypeStruct(q.shape, BF16),
        compiler_params=_params("arbitrary", "arbitrary", "arbitrary"),
        name="attention_latent" if ctx is not None else "attention_ctx",
    )(*args)


def _first_index(vals, target):
    idx = jnp.full(target.shape, float(len(vals) - 1), F32)
    for j in range(len(vals) - 2, -1, -1):
        idx = jnp.where(vals[j] == target, float(j), idx)
    return idx


def _route(lt):
    rows = [lt[i:i + 1, :] for i in range(N_EXPERT_GROUPS + N_EXPERTS)]
    gl = rows[:N_EXPERT_GROUPS]
    gmax = functools.reduce(jnp.maximum, gl)
    gidx = _first_index(gl, gmax)
    g_w = 1.0 / functools.reduce(lambda a, b: a + b, [jnp.exp(v - gmax) for v in gl])
    es = []
    for j in range(EXPERTS_PER_GROUP):
        sel = rows[N_EXPERT_GROUPS + (N_EXPERT_GROUPS - 1) * EXPERTS_PER_GROUP + j]
        for g in range(N_EXPERT_GROUPS - 2, -1, -1):
            sel = jnp.where(gidx == float(g), rows[N_EXPERT_GROUPS + g * EXPERTS_PER_GROUP + j], sel)
        es.append(sel)
    e1 = functools.reduce(jnp.maximum, es)
    i1 = _first_index(es, e1)
    rest = [jnp.where(i1 == float(j), -jnp.inf, es[j]) for j in range(EXPERTS_PER_GROUP)]
    e2 = functools.reduce(jnp.maximum, rest)
    i2 = _first_index(rest, e2)
    t = jnp.exp(e2 - e1)
    w1 = g_w / (1.0 + t)
    w2 = w1 * t
    lo = jnp.minimum(i1, i2)
    hi = jnp.maximum(i1, i2)
    w_lo = jnp.where(i1 < i2, w1, w2)
    w_hi = jnp.where(i1 < i2, w2, w1)
    pair = lo * (7.0 - lo) * 0.5 + (hi - lo - 1.0)
    return gidx * float(PAIRS_PER_GROUP) + pair, w_lo, w_hi


def _outproj_kernel(x_ref, f_ref, a_ref, mod_ref, n2_ref, wo_ref, wr_ref, br_ref, x1_ref, h2_ref, info_ref):
    d = x_ref.shape[1]
    tt = x_ref.shape[0]
    mix = _mm(f_ref[...].astype(BF16), wo_ref[:F_WIDTH, :])
    for g in range(N_KV_HEADS):
        lo = F_WIDTH + GROUP_WIDTH * g
        mix += _mm(a_ref[g], wo_ref[lo:lo + GROUP_WIDTH, :])
    x1 = x_ref[...] + mod_ref[0, 2:3, :] * mix
    x1_ref[...] = x1
    h2 = x1 * lax.rsqrt(jnp.mean(x1 * x1, axis=-1, keepdims=True) + EPS) * n2_ref[...]
    h2 = h2 * (1.0 + mod_ref[0, 4:5, :]) + mod_ref[0, 3:4, :]
    logits = _mm3(h2, wr_ref[...]) + br_ref[...]
    bucket, w_lo, w_hi = _route(logits.T)
    h2_ref[:, :d] = h2
    row = lax.broadcasted_iota(jnp.int32, (LANES, tt), 0)
    wts = jnp.where(row == 0, w_lo, jnp.where(row == 1, w_hi, 0.0))
    h2_ref[:, d:] = wts.T
    row8 = lax.broadcasted_iota(jnp.int32, (SUBLANES, tt), 0)
    info_ref[...] = jnp.where(row8 == 0, bucket, 0.0)


def _outproj(x, f_out, a_out, modl, n2, w_o, wr, br, batch, seq, latent):
    d = x.shape[1]
    tt = TOKEN_TILE
    nt = seq // tt
    t = batch * seq
    row = lambda b, i: (b * nt + i, 0)
    const = lambda b, i: (0, 0)
    mod_map = (lambda b, i: (b + 1, 0, 0)) if latent else (lambda b, i: (0, 0, 0))
    return pl.pallas_call(
        _outproj_kernel,
        grid=(batch, nt),
        in_specs=[
            pl.BlockSpec((tt, d), row),
            pl.BlockSpec((tt, F_WIDTH), row),
            pl.BlockSpec((N_KV_HEADS, tt, GROUP_WIDTH), lambda b, i: (0, b * nt + i, 0)),
            pl.BlockSpec((1, N_MOD, d), mod_map),
            pl.BlockSpec((1, d), const),
            pl.BlockSpec((F_WIDTH + Q_WIDTH, d), const),
            pl.BlockSpec((d, LANES), const),
            pl.BlockSpec((1, LANES), const),
        ],
        out_specs=[
            pl.BlockSpec((tt, d), row),
            pl.BlockSpec((tt, d + LANES), row),
            pl.BlockSpec((SUBLANES, tt), lambda b, i: (0, b * nt + i)),
        ],
        out_shape=[
            jax.ShapeDtypeStruct((t, d), F32),
            jax.ShapeDtypeStruct((t, d + LANES), F32),
            jax.ShapeDtypeStruct((SUBLANES, t), F32),
        ],
        compiler_params=_params("arbitrary", "arbitrary"),
        name="outproj_latent" if latent else "outproj_ctx",
    )(x, f_out, a_out, modl, n2, w_o, wr, br)


def _rank_kernel(b_ref, pos_ref, start_ref, *, n_chunks, ch):
    rows = lax.broadcasted_iota(jnp.int32, (BUCKET_ROWS, ch), 0).astype(F32)
    ones = jnp.ones((ch, LANES), BF16)
    upper = (lax.broadcasted_iota(jnp.int32, (ch, ch), 0) < lax.broadcasted_iota(jnp.int32, (ch, ch), 1)).astype(BF16)

    def onehot(i):
        off = pl.multiple_of(i * ch, ch)
        return (rows == b_ref[:, pl.ds(off, ch)]).astype(BF16)

    count = lax.fori_loop(0, n_chunks, lambda i, c: c + _mm(onehot(i), ones), jnp.zeros((BUCKET_ROWS, LANES), F32))
    n_tiles = jnp.floor((count + float(MOE_TILE - 1)) * (1.0 / MOE_TILE))
    lower = (lax.broadcasted_iota(jnp.int32, (BUCKET_ROWS, BUCKET_ROWS), 1)
             < lax.broadcasted_iota(jnp.int32, (BUCKET_ROWS, BUCKET_ROWS), 0)).astype(BF16)
    start_tiles = _mm(lower, n_tiles.astype(BF16))
    start_ref[...] = start_tiles
    start_rows = start_tiles * float(MOE_TILE)

    def body(i, carry):
        oh = onehot(i)
        prefix = _mm(oh, upper)
        base = jnp.concatenate([start_rows + carry] * (ch // LANES), axis=1)
        pos = jnp.sum(oh.astype(F32) * (prefix + base), axis=0, keepdims=True)
        off = pl.multiple_of(i * ch, ch)
        pos_ref[:, pl.ds(off, ch)] = pos.astype(jnp.int32)
        return carry + _mm(oh, ones)

    lax.fori_loop(0, n_chunks, body, jnp.zeros((BUCKET_ROWS, LANES), F32))


def _rank(bucket_row):
    t = bucket_row.shape[1]
    ch = RANK_CHUNK
    return pl.pallas_call(
        functools.partial(_rank_kernel, n_chunks=t // ch, ch=ch),
        out_shape=[jax.ShapeDtypeStruct((1, t), jnp.int32), jax.ShapeDtypeStruct((BUCKET_ROWS, LANES), F32)],
        compiler_params=pltpu.CompilerParams(vmem_limit_bytes=VMEM_LIMIT),
        name="bucket_rank",
    )(bucket_row)


def _dispatch_kernel(*refs, tile_starts):
    n_src = len(tile_starts) - 1
    zero_ref, pos_ref = refs[:2]
    src_refs = refs[2:2 + n_src]
    dst_ref, zeros_sc, sem, zsem = refs[2 + n_src:]
    i = pl.program_id(0)
    tt = pos_ref.shape[2]

    def row_block(ref, start, rows):
        return ref.at[pl.ds(pl.multiple_of(start, SUBLANES), rows)]

    def zero_copy(t):
        return pltpu.make_async_copy(zeros_sc, row_block(dst_ref, t * MOE_TILE, MOE_TILE), zsem)

    @pl.when(i == 0)
    def _():
        zeros_sc[...] = jnp.zeros(zeros_sc.shape, F32)
        for wait in (False, True):
            def tile(t, c, wait=wait):
                @pl.when(zero_ref[t] == 1)
                def _():
                    zero_copy(t).wait() if wait else zero_copy(t).start()
                return c
            lax.fori_loop(0, zero_ref.shape[0], tile, 0)

    for k in range(n_src):
        @pl.when((i >= tile_starts[k]) & (i < tile_starts[k + 1]))
        def _(src_ref=src_refs[k], first_tile=tile_starts[k]):
            def body(r, c):
                row = (i - first_tile) * tt + r
                pltpu.make_async_copy(src_ref.at[pl.ds(row, 1)], dst_ref.at[pl.ds(pos_ref[0, 0, r], 1)], sem).start()
                return c
            lax.fori_loop(0, tt, body, 0, unroll=8)

    def wait_tile():
        pltpu.make_async_copy(row_block(src_refs[0], 0, tt), row_block(dst_ref, 0, tt), sem).wait()

    @pl.when(i > 0)
    def _():
        wait_tile()

    @pl.when(i == pl.num_programs(0) - 1)
    def _():
        wait_tile()


def _dispatch(pos3, sources, zero_mask, n_rows):
    nt, _, tt = pos3.shape
    width = sources[0].shape[1]
    tile_starts = [0]
    for src in sources:
        tile_starts.append(tile_starts[-1] + src.shape[0] // tt)
    any_spec = pl.BlockSpec(memory_space=pl.ANY)
    grid_spec = pltpu.PrefetchScalarGridSpec(
        num_scalar_prefetch=1,
        grid=(nt,),
        in_specs=[pl.BlockSpec((1, 1, tt), lambda i, z: (i, 0, 0), memory_space=pltpu.SMEM)] + [any_spec] * len(sources),
        out_specs=any_spec,
        scratch_shapes=[pltpu.VMEM((MOE_TILE, width), F32), pltpu.SemaphoreType.DMA(()), pltpu.SemaphoreType.DMA(())],
    )
    return pl.pallas_call(
        functools.partial(_dispatch_kernel, tile_starts=tuple(tile_starts)),
        grid_spec=grid_spec,
        out_shape=jax.ShapeDtypeStruct((n_rows, width), F32),
        compiler_params=_params("arbitrary"),
        name="moe_dispatch",
    )(zero_mask, pos3, *sources)


def _moe_kernel(ea_ref, eb_ref, valid_ref, h_ref, wga, wua, wda, wgb, wub, wdb, o_ref):
    del ea_ref, eb_ref
    i = pl.program_id(0)
    d = o_ref.shape[1]

    @pl.when(valid_ref[i] == 1)
    def _():
        hb = h_ref[:, :d].astype(BF16)

        def expert(wg, wu, wd):
            hid = _silu(_mm(hb, wg[0])) * _mm(hb, wu[0])
            return _mm(hid.astype(BF16), wd[0])

        ya = expert(wga, wua, wda)
        yb = expert(wgb, wub, wdb)
        o_ref[...] = h_ref[:, d:d + 1] * ya + h_ref[:, d + 1:d + 2] * yb

    @pl.when(valid_ref[i] == 0)
    def _():
        o_ref[...] = jnp.zeros(o_ref.shape, F32)


def _moe(ea, eb, valid, h2s, wg, wu, wd):
    n_tiles = ea.shape[0]
    tm = MOE_TILE
    d = wd.shape[2]
    de = wd.shape[1]
    a_map = lambda i, ea, eb, valid: (ea[i], 0, 0)
    b_map = lambda i, ea, eb, valid: (eb[i], 0, 0)
    row = lambda i, ea, eb, valid: (i, 0)
    grid_spec = pltpu.PrefetchScalarGridSpec(
        num_scalar_prefetch=3,
        grid=(n_tiles,),
        in_specs=[
            pl.BlockSpec((tm, h2s.shape[1]), row),
            pl.BlockSpec((1, d, de), a_map), pl.BlockSpec((1, d, de), a_map), pl.BlockSpec((1, de, d), a_map),
            pl.BlockSpec((1, d, de), b_map), pl.BlockSpec((1, d, de), b_map), pl.BlockSpec((1, de, d), b_map),
        ],
        out_specs=pl.BlockSpec((tm, d), row),
    )
    return pl.pallas_call(
        _moe_kernel,
        grid_spec=grid_spec,
        out_shape=jax.ShapeDtypeStruct((n_tiles * tm, d), F32),
        compiler_params=_params("arbitrary"),
        name="moe_experts",
    )(ea, eb, valid, h2s, wg, wu, wd, wg, wu, wd)


def _combine_kernel(pos_ref, next_pos_ref, x_ref, mod_ref, ys_ref, o_ref, buf, sem):
    i = pl.program_id(0)
    tt = x_ref.shape[0]
    slot = i % 2

    def issue(p_ref, s):
        def body(r, c):
            pltpu.make_async_copy(ys_ref.at[pl.ds(p_ref[0, 0, r], 1)], buf.at[s, pl.ds(r, 1)], sem.at[s]).start()
            return c
        lax.fori_loop(0, tt, body, 0, unroll=8)

    @pl.when(i == 0)
    def _():
        issue(pos_ref, 0)

    @pl.when(i + 1 < pl.num_programs(0))
    def _():
        issue(next_pos_ref, 1 - slot)

    pltpu.make_async_copy(ys_ref.at[pl.ds(0, tt)], buf.at[slot], sem.at[slot]).wait()
    o_ref[...] = x_ref[...] + mod_ref[0, 5:6, :] * buf[slot]


def _combine(pos3, x1, modl, ys, batch, seq, latent):
    d = x1.shape[1]
    tt = TOKEN_TILE
    nt = seq // tt
    mod_map = (lambda i: (i // nt + 1, 0, 0)) if latent else (lambda i: (0, 0, 0))
    return pl.pallas_call(
        _combine_kernel,
        grid=(batch * nt,),
        in_specs=[
            pl.BlockSpec((1, 1, tt), lambda i: (i, 0, 0), memory_space=pltpu.SMEM),
            pl.BlockSpec((1, 1, tt), lambda i: (jnp.minimum(i + 1, batch * nt - 1), 0, 0), memory_space=pltpu.SMEM),
            pl.BlockSpec((tt, d), lambda i: (i, 0)),
            pl.BlockSpec((1, N_MOD, d), mod_map),
            pl.BlockSpec(memory_space=pl.ANY),
        ],
        out_specs=pl.BlockSpec((tt, d), lambda i: (i, 0)),
        out_shape=jax.ShapeDtypeStruct(x1.shape, F32),
        scratch_shapes=[pltpu.VMEM((2, tt, d), F32), pltpu.SemaphoreType.DMA((2,))],
        compiler_params=_params("arbitrary"),
        name="moe_combine_latent" if latent else "moe_combine_ctx",
    )(pos3, pos3, x1, modl, ys)


def _rope_tables(seq):
    rows = seq // GRID_W
    row = jnp.repeat(jnp.arange(rows, dtype=F32), GRID_W)
    col = jnp.tile(jnp.arange(GRID_W, dtype=F32), rows)
    inv = ROPE_THETA ** (-jnp.arange(0, ROPE_AXIS_DIM, 2, dtype=F32) / ROPE_AXIS_DIM)
    cos_parts, sin_parts = [], []
    for pos in (row, col):
        ang = pos[:, None] * inv
        cos_parts += [jnp.cos(ang), jnp.cos(ang)]
        sin_parts += [-jnp.sin(ang), jnp.sin(ang)]
    cos = jnp.concatenate(cos_parts, axis=-1)
    sin = jnp.concatenate(sin_parts, axis=-1)
    return jnp.tile(cos, (1, LANES // HEAD_DIM)), jnp.tile(sin, (1, LANES // HEAD_DIM))


def _tile_tables(start_tiles, n_tiles):
    st = start_tiles[:N_BUCKETS + 1, 0].astype(jnp.int32)
    tiles = jnp.arange(n_tiles, dtype=jnp.int32)
    total = st[N_BUCKETS]
    bucket = jnp.sum((jnp.minimum(tiles, total - 1)[:, None] >= st[None, 1:]).astype(jnp.int32), axis=1)
    bucket = jnp.minimum(bucket, N_BUCKETS - 1)
    ends_bucket = jnp.any((tiles + 1)[:, None] == st[None, 1:], axis=1)
    may_pad = ((tiles >= total) | ends_bucket).astype(jnp.int32)
    pairs = [(a, b) for a in range(EXPERTS_PER_GROUP) for b in range(a + 1, EXPERTS_PER_GROUP)]
    lo = jnp.asarray([p[0] for p in pairs], jnp.int32)
    hi = jnp.asarray([p[1] for p in pairs], jnp.int32)
    grp = bucket // PAIRS_PER_GROUP
    pair = bucket % PAIRS_PER_GROUP
    ea = grp * EXPERTS_PER_GROUP + lo[pair]
    eb = grp * EXPERTS_PER_GROUP + hi[pair]
    return ea, eb, (tiles < total).astype(jnp.int32), may_pad


def _ctx_kv(cache_k_l, cache_v_l):
    b, p = cache_k_l.shape[:2]
    ckt = cache_k_l.transpose(0, 2, 3, 1).astype(BF16)
    v = cache_v_l.transpose(0, 2, 1, 3)
    pad = jnp.zeros((b, N_KV_HEADS, p, LANES - HEAD_DIM), F32).at[..., 0].set(1.0)
    return ckt, jnp.concatenate([v, pad], axis=-1).astype(BF16)


def kernel(x_prompt, x_sample, cache_k, cache_v, c, c_ctx, norm1_g, norm2_g, w_mod, b_mod, w_in, w_fourier,
           q_norm_g, k_norm_g, w_out, w_router_group, b_router_group, w_router_expert, b_router_expert,
           w_gate, w_up, w_down):
    batch, seq, d = x_prompt.shape
    dec_batch, dec_seq, _ = x_sample.shape
    depth = w_in.shape[0]
    streams = [
        dict(latent=False, batch=batch, seq=seq, x=x_prompt.reshape(batch * seq, d)),
        dict(latent=True, batch=dec_batch, seq=dec_seq, x=x_sample.reshape(dec_batch * dec_seq, d)),
    ]
    t_total = sum(s["batch"] * s["seq"] for s in streams)
    n_sorted_tiles = t_total // MOE_TILE + N_BUCKETS

    cond_rows = 2 * SUBLANES
    cs = jnp.zeros((cond_rows, d), F32).at[0].set(c_ctx).at[1:1 + dec_batch].set(c)
    mod = _adaln_mod(cs, w_mod, b_mod).reshape(depth, cond_rows, N_MOD, d)

    seg = np.arange(LANES) // HEAD_DIM
    bd = jnp.asarray((seg[:, None] == seg[None, :]).astype(np.float32) / HEAD_DIM, BF16)
    rope_tabs = _rope_tables(dec_seq)
    kc_layers, vc_layers = [], []

    for l in range(depth):
        modl = mod[l]
        w_in_l = w_in[l].astype(BF16)
        w_o_l = w_out[l].astype(BF16)
        qg = jnp.tile(q_norm_g[l], N_Q_HEADS)[None, :]
        kg = jnp.tile(k_norm_g[l], N_KV_HEADS)[None, :]
        n_router = N_EXPERT_GROUPS + N_EXPERTS
        wr = jnp.concatenate([w_router_group[l], w_router_expert[l], jnp.zeros((d, LANES - n_router), F32)], axis=1)
        br = jnp.concatenate([b_router_group[l], b_router_expert[l], jnp.zeros((LANES - n_router,), F32)])[None, :]
        wg_l, wu_l, wd_l = w_gate[l].astype(BF16), w_up[l].astype(BF16), w_down[l].astype(BF16)

        for s in streams:
            b_, n_, lat = s["batch"], s["seq"], s["latent"]
            res = _inproj(s["x"], modl, norm1_g[l][None, :], w_in_l, qg, kg, bd, rope_tabs, b_, n_, lat)
            f_in, q, kt, vx = res[:4]
            if lat:
                ctx = _ctx_kv(cache_k[:, l], cache_v[:, l])
            else:
                ctx = None
                kc_layers.append(res[4])
                vc_layers.append(res[5])
            fourier = _fourier_fft if _uses_fft(n_) else _fourier_dense
            f_out = fourier(f_in, w_fourier[l].astype(BF16), b_, n_)
            a_out = _attention(q, kt, vx, ctx, b_, n_)
            s["x1"], s["h2e"], s["info"] = _outproj(s["x"], f_out, a_out, modl, norm2_g[l][None, :], w_o_l, wr, br,
                                                    b_, n_, lat)

        bucket_row = jnp.concatenate([s["info"][0:1] for s in streams], axis=1)
        pos, start_tiles = _rank(bucket_row)
        ea, eb, valid, may_pad = _tile_tables(start_tiles, n_sorted_tiles)

        pos3 = pos.reshape(t_total // TOKEN_TILE, 1, TOKEN_TILE)
        h2s = _dispatch(pos3, [s["h2e"] for s in streams], may_pad, n_sorted_tiles * MOE_TILE)
        off = 0
        for s in streams:
            n_tok_tiles = s["batch"] * s["seq"] // TOKEN_TILE
            s["pos3"] = pos3[off:off + n_tok_tiles]
            off += n_tok_tiles
        ys = _moe(ea, eb, valid, h2s, wg_l, wu_l, wd_l)
        for s in streams:
            s["x"] = _combine(s["pos3"], s["x1"], modl, ys, s["batch"], s["seq"], s["latent"])

    y_prompt = streams[0]["x"].reshape(batch, seq, d)
    y_sample = streams[1]["x"].reshape(dec_batch, dec_seq, d)
    cache_shape = (batch, depth, seq, N_KV_HEADS, HEAD_DIM)
    new_k = jnp.stack([k.reshape(batch, seq, KV_WIDTH) for k in kc_layers], axis=1).reshape(cache_shape)
    new_v = jnp.stack([v.reshape(batch, seq, KV_WIDTH) for v in vc_layers], axis=1).reshape(cache_shape)
    return (y_prompt, y_sample, new_k, new_v)
```

```python
import functools

import numpy as np
import jax
import jax.numpy as jnp
from jax import lax
from jax.experimental import pallas as pl
from jax.experimental.pallas import tpu as pltpu

F32 = jnp.float32
BF16 = jnp.bfloat16

HEAD_DIM = 64
N_Q_HEADS = 12
N_KV_HEADS = 4
KV_GROUP = N_Q_HEADS // N_KV_HEADS
Q_WIDTH = N_Q_HEADS * HEAD_DIM
GROUP_WIDTH = KV_GROUP * HEAD_DIM
KV_WIDTH = N_KV_HEADS * HEAD_DIM
F_GROUPS = 4
F_GROUP_DIM = 64
F_WIDTH = F_GROUPS * F_GROUP_DIM
GRID_W = 64
ROPE_AXIS_DIM = HEAD_DIM // 2
ROPE_HALF = ROPE_AXIS_DIM // 2
ROPE_THETA = 10000.0
N_EXPERT_GROUPS = 4
EXPERTS_PER_GROUP = 4
N_EXPERTS = N_EXPERT_GROUPS * EXPERTS_PER_GROUP
PAIRS_PER_GROUP = 6
N_BUCKETS = N_EXPERT_GROUPS * PAIRS_PER_GROUP
N_MOD = 6
EPS = 1e-6
ATTN_SCALE = HEAD_DIM ** -0.5
LOG2_E = 1.4426950408889634

LANES = 128
SUBLANES = 8

TOKEN_TILE = 256
SUB_TILE = 256
MOE_TILE = 256
ATTN_Q_TILE = 512
ATTN_K_TILE = 512
FFT_TOKEN_TILE = 512
RANK_CHUNK = 512
BUCKET_ROWS = 32
VMEM_LIMIT = 48 * 1024 * 1024
NEG_BIG = -1e30


def _split(x):
    hi = x.astype(BF16)
    lo = (x - hi.astype(F32)).astype(BF16)
    return hi, lo


def _mm(a, b):
    return jnp.dot(a, b, preferred_element_type=F32)


def _mm3(a, b):
    ah, al = _split(a)
    bh, bl = _split(b)
    return _mm(ah, bh) + (_mm(ah, bl) + _mm(al, bh))


def _silu(x):
    return x / (1.0 + jnp.exp(-x))


def _uses_fft(seq):
    return seq > TOKEN_TILE


def _params(*sem):
    return pltpu.CompilerParams(dimension_semantics=sem, vmem_limit_bytes=VMEM_LIMIT)


def _mod_kernel(c_ref, w_ref, b_ref, o_ref):
    o_ref[0] = _mm3(_silu(c_ref[...]), w_ref[0]) + b_ref[0]


def _adaln_mod(cs, w_mod, b_mod):
    n_layers, d, m = w_mod.shape
    r = cs.shape[0]
    tn = d
    return pl.pallas_call(
        _mod_kernel,
        grid=(n_layers, m // tn),
        in_specs=[
            pl.BlockSpec((r, d), lambda l, j: (0, 0)),
            pl.BlockSpec((1, d, tn), lambda l, j: (l, 0, j)),
            pl.BlockSpec((1, 1, tn), lambda l, j: (l, 0, j)),
        ],
        out_specs=pl.BlockSpec((1, r, tn), lambda l, j: (l, 0, j)),
        out_shape=jax.ShapeDtypeStruct((n_layers, r, m), F32),
        compiler_params=_params("arbitrary", "arbitrary"),
        name="adaln_mod",
    )(cs, w_mod, b_mod.reshape(n_layers, 1, m))


def _head_norm(z, gain, bd):
    outs = []
    for j in range(z.shape[1] // LANES):
        zj = z[:, LANES * j:LANES * (j + 1)]
        hi, lo = _split(zj * zj)
        msq = _mm(jnp.concatenate([hi, lo], axis=1), bd)
        outs.append(zj * lax.rsqrt(msq + EPS))
    return jnp.concatenate(outs, axis=1) * gain


def _rope(z, cos, sin):
    lane = lax.broadcasted_iota(jnp.int32, (z.shape[0], LANES), 1)
    first_half = (lane % ROPE_AXIS_DIM) < ROPE_HALF
    outs = []
    for j in range(z.shape[1] // LANES):
        zj = z[:, LANES * j:LANES * (j + 1)]
        partner = jnp.where(first_half, pltpu.roll(zj, LANES - ROPE_HALF, 1), pltpu.roll(zj, ROPE_HALF, 1))
        outs.append(zj * cos + partner * sin)
    return jnp.concatenate(outs, axis=1)


def _inproj_kernel(*refs, latent, grid_major):
    if latent:
        (x_ref, mod_ref, n1_ref, w_ref, qg_ref, kg_ref, bd_ref, cos_ref, sin_ref,
         f_ref, q_ref, kt_ref, vx_ref) = refs
    else:
        (x_ref, mod_ref, n1_ref, w_ref, qg_ref, kg_ref, bd_ref,
         f_ref, q_ref, kt_ref, vx_ref, kc_ref, vc_ref) = refs
    bd = bd_ref[...]
    for r0 in range(0, x_ref.shape[0], SUB_TILE):
        rows = slice(r0, r0 + SUB_TILE)
        x = x_ref[rows, :]
        h = x * lax.rsqrt(jnp.mean(x * x, axis=-1, keepdims=True) + EPS) * n1_ref[...]
        h = h * (1.0 + mod_ref[0, 1:2, :]) + mod_ref[0, 0:1, :]
        p = _mm(h.astype(BF16), w_ref[...])
        if grid_major:
            for a in range(SUB_TILE // GRID_W):
                f_ref[0, :, r0 // GRID_W + a, :] = p[GRID_W * a:GRID_W * (a + 1), :F_WIDTH]
        else:
            f_ref[rows, :] = p[:, :F_WIDTH]
        q = _head_norm(p[:, F_WIDTH:F_WIDTH + Q_WIDTH], qg_ref[...], bd)
        k = _head_norm(p[:, F_WIDTH + Q_WIDTH:F_WIDTH + Q_WIDTH + KV_WIDTH], kg_ref[...], bd)
        v = p[:, F_WIDTH + Q_WIDTH + KV_WIDTH:]
        if latent:
            cos = cos_ref[rows, :]
            sin = sin_ref[rows, :]
            q = _rope(q, cos, sin)
            k = _rope(k, cos, sin)
        else:
            kc_ref[rows, :] = k
            vc_ref[rows, :] = v
        qs = (q * (ATTN_SCALE * LOG2_E)).astype(BF16)
        for g in range(N_KV_HEADS):
            q_ref[g, rows, :] = qs[:, GROUP_WIDTH * g:GROUP_WIDTH * (g + 1)]
        kt_ref[0, :, :, rows] = k.T.reshape(N_KV_HEADS, HEAD_DIM, SUB_TILE).astype(BF16)
        lane = lax.broadcasted_iota(jnp.int32, (SUB_TILE, LANES), 1)
        ones_col = (lane == HEAD_DIM).astype(F32)
        for j in range(KV_WIDTH // LANES):
            vj = v[:, LANES * j:LANES * (j + 1)]
            vx_ref[0, 2 * j, rows, :] = jnp.where(lane < HEAD_DIM, vj, ones_col).astype(BF16)
            vx_ref[0, 2 * j + 1, rows, :] = jnp.where(lane < HEAD_DIM, pltpu.roll(vj, HEAD_DIM, 1),
                                                      ones_col).astype(BF16)


def _inproj(x, modl, n1, w_in, qg, kg, bd, rope_tabs, batch, seq, latent):
    d = x.shape[1]
    grid_major = _uses_fft(seq)
    tt = FFT_TOKEN_TILE if grid_major else TOKEN_TILE
    nt = seq // tt
    in_w = w_in.shape[1]
    row = lambda b, i: (b * nt + i, 0)
    const = lambda b, i: (0, 0)
    mod_map = (lambda b, i: (b + 1, 0, 0)) if latent else (lambda b, i: (0, 0, 0))
    in_specs = [
        pl.BlockSpec((tt, d), row),
        pl.BlockSpec((1, N_MOD, d), mod_map),
        pl.BlockSpec((1, d), const),
        pl.BlockSpec((d, in_w), const),
        pl.BlockSpec((1, Q_WIDTH), const),
        pl.BlockSpec((1, KV_WIDTH), const),
        pl.BlockSpec((2 * LANES, LANES), const),
    ]
    args = [x, modl, n1, w_in, qg, kg, bd]
    if latent:
        in_specs += [pl.BlockSpec((tt, LANES), lambda b, i: (i, 0))] * 2
        args += list(rope_tabs)
    t = batch * seq
    out_shape = [
        jax.ShapeDtypeStruct((batch, GRID_W, seq // GRID_W, F_WIDTH) if grid_major else (t, F_WIDTH), F32),
        jax.ShapeDtypeStruct((N_KV_HEADS, t, GROUP_WIDTH), BF16),
        jax.ShapeDtypeStruct((batch, N_KV_HEADS, HEAD_DIM, seq), BF16),
        jax.ShapeDtypeStruct((batch, N_KV_HEADS, seq, LANES), BF16),
    ]
    out_specs = [
        (pl.BlockSpec((1, GRID_W, tt // GRID_W, F_WIDTH), lambda b, i: (b, 0, i, 0)) if grid_major
         else pl.BlockSpec((tt, F_WIDTH), row)),
        pl.BlockSpec((N_KV_HEADS, tt, GROUP_WIDTH), lambda b, i: (0, b * nt + i, 0)),
        pl.BlockSpec((1, N_KV_HEADS, HEAD_DIM, tt), lambda b, i: (b, 0, 0, i)),
        pl.BlockSpec((1, N_KV_HEADS, tt, LANES), lambda b, i: (b, 0, i, 0)),
    ]
    if not latent:
        out_shape += [jax.ShapeDtypeStruct((t, KV_WIDTH), F32)] * 2
        out_specs += [pl.BlockSpec((tt, KV_WIDTH), row)] * 2
    return pl.pallas_call(
        functools.partial(_inproj_kernel, latent=latent, grid_major=grid_major),
        grid=(batch, nt),
        in_specs=in_specs,
        out_specs=out_specs,
        out_shape=out_shape,
        compiler_params=_params("arbitrary", "arbitrary"),
        name="inproj_latent" if latent else "inproj_ctx",
    )(*args)


def _dft_cos_sin(n):
    idx = np.arange(n, dtype=np.int64)
    ang = 2.0 * np.pi * ((idx[:, None] * idx[None, :]) % n).astype(np.float64) / n
    return np.cos(ang), np.sin(ang)


def _block_diag(m, reps):
    n = m.shape[0]
    out = np.zeros((n * reps, n * reps), m.dtype)
    for g in range(reps):
        out[g * n:(g + 1) * n, g * n:(g + 1) * n] = m
    return out


def _mxu_const(x):
    return jnp.asarray(x, F32).astype(BF16)


def _channel_dft():
    c, s = _dft_cos_sin(F_GROUP_DIM)
    return _mxu_const(_block_diag(c, F_GROUPS)), _mxu_const(_block_diag(s, F_GROUPS))


def _fourier_dense_kernel(z_ref, cn_ref, sn_ref, bdc_ref, bds_ref, wf_ref, o_ref, *, scale):
    z = z_ref[...].astype(BF16)
    zc = _mm(z, bdc_ref[...]).astype(BF16)
    zs = _mm(z, bds_ref[...]).astype(BF16)
    y = (_mm(cn_ref[...], zc) - _mm(sn_ref[...], zs)) * scale
    o_ref[...] = _mm(y.astype(BF16), wf_ref[...])


def _fourier_dense(f_in, w_f, batch, seq):
    cn, sn = _dft_cos_sin(seq)
    bdc, bds = _channel_dft()
    const = lambda b: (0, 0)
    return pl.pallas_call(
        functools.partial(_fourier_dense_kernel, scale=float((seq * F_GROUP_DIM) ** -0.5)),
        grid=(batch,),
        in_specs=[
            pl.BlockSpec((seq, F_WIDTH), lambda b: (b, 0)),
            pl.BlockSpec((seq, seq), const),
            pl.BlockSpec((seq, seq), const),
            pl.BlockSpec((F_WIDTH, F_WIDTH), const),
            pl.BlockSpec((F_WIDTH, F_WIDTH), const),
            pl.BlockSpec((F_WIDTH, F_WIDTH), const),
        ],
        out_specs=pl.BlockSpec((seq, F_WIDTH), lambda b: (b, 0)),
        out_shape=jax.ShapeDtypeStruct((batch * seq, F_WIDTH), F32),
        compiler_params=_params("arbitrary"),
        name="fourier_dense",
    )(f_in, _mxu_const(cn), _mxu_const(sn), bdc, bds, w_f)


def _fft_kernel(z_ref, cs_ref, g_ref, bdc_ref, bds_ref, wf_ref, o_ref, y_sc, *, scale, r, cb):
    def rows(b, carry):
        yb = _mm(cs_ref[...], z_ref[0, b].astype(BF16))
        y_sc[0, :, b, :] = yb[:r]
        y_sc[1, :, b, :] = yb[r:]
        return carry

    lax.fori_loop(0, GRID_W, rows, 0, unroll=8)

    def cols(j, carry):
        xr, xi = [], []
        for cc in range(cb):
            c = j * cb + cc
            ystack = jnp.concatenate([y_sc[0, c], y_sc[1, c]], axis=0).astype(BF16)
            xc = _mm(g_ref[c], ystack)
            xr.append(xc[:GRID_W])
            xi.append(xc[GRID_W:])
        xr = jnp.concatenate(xr, axis=0).astype(BF16)
        xi = jnp.concatenate(xi, axis=0).astype(BF16)
        o = (_mm(xr, bdc_ref[...]) + _mm(xi, bds_ref[...])) * scale
        o = _mm(o.astype(BF16), wf_ref[...])
        for cc in range(cb):
            o_ref[0, :, j * cb + cc, :] = o[GRID_W * cc:GRID_W * (cc + 1)]
        return carry

    lax.fori_loop(0, r // cb, cols, 0, unroll=2)


def _fourier_fft(f_in, w_f, batch, seq):
    r = seq // GRID_W
    cr, sr = _dft_cos_sin(r)
    cs = _mxu_const(np.concatenate([cr, -sr], axis=0))
    bb = np.arange(GRID_W, dtype=np.int64)
    dd = np.arange(GRID_W, dtype=np.int64)
    cc = np.arange(r, dtype=np.int64)
    ang = 2.0 * np.pi * ((bb[None, None, :] * (r * dd[None, :, None] + cc[:, None, None])) % seq) / seq
    gr, gi = np.cos(ang), -np.sin(ang)
    g = _mxu_const(np.concatenate([np.concatenate([gr, -gi], axis=2),
                                   np.concatenate([gi, gr], axis=2)], axis=1))
    bdc, bds = _channel_dft()
    cb = min(r, 8)
    const = lambda b: (0, 0)
    out = pl.pallas_call(
        functools.partial(_fft_kernel, scale=float((seq * F_GROUP_DIM) ** -0.5), r=r, cb=cb),
        grid=(batch,),
        in_specs=[
            pl.BlockSpec((1, GRID_W, r, F_WIDTH), lambda b: (b, 0, 0, 0)),
            pl.BlockSpec((2 * r, r), const),
            pl.BlockSpec((r, 2 * GRID_W, 2 * GRID_W), lambda b: (0, 0, 0)),
            pl.BlockSpec((F_WIDTH, F_WIDTH), const),
            pl.BlockSpec((F_WIDTH, F_WIDTH), const),
            pl.BlockSpec((F_WIDTH, F_WIDTH), const),
        ],
        out_specs=pl.BlockSpec((1, GRID_W, r, F_WIDTH), lambda b: (b, 0, 0, 0)),
        out_shape=jax.ShapeDtypeStruct((batch, GRID_W, r, F_WIDTH), F32),
        scratch_shapes=[pltpu.VMEM((2, r, GRID_W, F_WIDTH), F32)],
        compiler_params=_params("arbitrary"),
        name="fourier_fft",
    )(f_in, cs, g, bdc, bds, w_f)
    return out.reshape(batch * seq, F_WIDTH)


def _attn_kernel(*refs, tk, n_chunks, has_ctx):
    if has_ctx:
        q_ref, kt_ref, vx_ref, ckt_ref, cvx_ref, o_ref = refs
    else:
        q_ref, kt_ref, vx_ref, o_ref = refs
    tq = q_ref.shape[1]
    q = q_ref[0]
    qg = jnp.concatenate([q[:, HEAD_DIM * h:HEAD_DIM * (h + 1)] for h in range(KV_GROUP)], axis=0)
    m_rows = KV_GROUP * tq

    def step(kt, vv, carry):
        m, acc = carry
        s = _mm(qg, kt)
        tile_max = functools.reduce(
            jnp.maximum, [s[:, LANES * j:LANES * (j + 1)] for j in range(s.shape[1] // LANES)])
        m_new = jnp.maximum(m, jnp.max(tile_max, axis=-1, keepdims=True))
        p = jnp.exp2(s - m_new).astype(BF16)
        return m_new, acc * jnp.exp2(m - m_new) + _mm(p, vv)

    carry = (jnp.full((m_rows, 1), NEG_BIG, F32), jnp.zeros((m_rows, LANES), F32))
    if has_ctx:
        carry = step(ckt_ref[0, 0], cvx_ref[0, 0], carry)
    for i in range(n_chunks):
        carry = step(kt_ref[0, 0, :, tk * i:tk * (i + 1)], vx_ref[0, 0, tk * i:tk * (i + 1), :], carry)
    acc = carry[1]
    o = acc[:, :HEAD_DIM] / acc[:, HEAD_DIM:HEAD_DIM + 1]
    o_ref[0] = jnp.concatenate([o[h * tq:(h + 1) * tq] for h in range(KV_GROUP)], axis=1).astype(BF16)


def _attention(q, kt, vx, ctx, batch, seq):
    tq = min(ATTN_Q_TILE, seq)
    tk = min(ATTN_K_TILE, seq)
    nq = seq // tq
    q_spec = pl.BlockSpec((1, tq, GROUP_WIDTH), lambda b, g, i: (g, b * nq + i, 0))
    in_specs = [
        q_spec,
        pl.BlockSpec((1, 1, HEAD_DIM, seq), lambda b, g, i: (b, g, 0, 0)),
        pl.BlockSpec((1, 1, seq, LANES), lambda b, g, i: (b, g, 0, 0)),
    ]
    args = [q, kt, vx]
    if ctx is not None:
        ckt, cvx = ctx
        past = ckt.shape[-1]
        in_specs += [
            pl.BlockSpec((1, 1, HEAD_DIM, past), lambda b, g, i: (b, g, 0, 0)),
            pl.BlockSpec((1, 1, past, LANES), lambda b, g, i: (b, g, 0, 0)),
        ]
        args += [ckt, cvx]
    return pl.pallas_call(
        functools.partial(_attn_kernel, tk=tk, n_chunks=seq // tk, has_ctx=ctx is not None),
        grid=(batch, N_KV_HEADS, nq),
        in_specs=in_specs,
        out_specs=q_spec,
        out_shape=jax.ShapeDtypeStruct(q.shape, BF16),
        compiler_params=_params("arbitrary", "arbitrary", "arbitrary"),
        name="attention_latent" if ctx is not None else "attention_ctx",
    )(*args)


def _first_index(vals, target):
    idx = jnp.full(target.shape, float(len(vals) - 1), F32)
    for j in range(len(vals) - 2, -1, -1):
        idx = jnp.where(vals[j] == target, float(j), idx)
    return idx


def _route(lt):
    rows = [lt[i:i + 1, :] for i in range(N_EXPERT_GROUPS + N_EXPERTS)]
    gl = rows[:N_EXPERT_GROUPS]
    gmax = functools.reduce(jnp.maximum, gl)
    gidx = _first_index(gl, gmax)
    g_w = 1.0 / functools.reduce(lambda a, b: a + b, [jnp.exp(v - gmax) for v in gl])
    es = []
    for j in range(EXPERTS_PER_GROUP):
        sel = rows[N_EXPERT_GROUPS + (N_EXPERT_GROUPS - 1) * EXPERTS_PER_GROUP + j]
        for g in range(N_EXPERT_GROUPS - 2, -1, -1):
            sel = jnp.where(gidx == float(g), rows[N_EXPERT_GROUPS + g * EXPERTS_PER_GROUP + j], sel)
        es.append(sel)
    e1 = functools.reduce(jnp.maximum, es)
    i1 = _first_index(es, e1)
    rest = [jnp.where(i1 == float(j), -jnp.inf, es[j]) for j in range(EXPERTS_PER_GROUP)]
    e2 = functools.reduce(jnp.maximum, rest)
    i2 = _first_index(rest, e2)
    t = jnp.exp(e2 - e1)
    w1 = g_w / (1.0 + t)
    w2 = w1 * t
    lo = jnp.minimum(i1, i2)
    hi = jnp.maximum(i1, i2)
    w_lo = jnp.where(i1 < i2, w1, w2)
    w_hi = jnp.where(i1 < i2, w2, w1)
    pair = lo * (7.0 - lo) * 0.5 + (hi - lo - 1.0)
    return gidx * float(PAIRS_PER_GROUP) + pair, w_lo, w_hi


def _outproj_kernel(x_ref, f_ref, a_ref, mod_ref, n2_ref, wo_ref, wr_ref, br_ref, x1_ref, h2_ref, info_ref):
    d = x_ref.shape[1]
    for r0 in range(0, x_ref.shape[0], SUB_TILE):
        rows = slice(r0, r0 + SUB_TILE)
        mix = _mm(f_ref[rows, :].astype(BF16), wo_ref[:F_WIDTH, :])
        for g in range(N_KV_HEADS):
            lo = F_WIDTH + GROUP_WIDTH * g
            mix += _mm(a_ref[g, rows, :], wo_ref[lo:lo + GROUP_WIDTH, :])
        x1 = x_ref[rows, :] + mod_ref[0, 2:3, :] * mix
        x1_ref[rows, :] = x1
        h2 = x1 * lax.rsqrt(jnp.mean(x1 * x1, axis=-1, keepdims=True) + EPS) * n2_ref[...]
        h2 = h2 * (1.0 + mod_ref[0, 4:5, :]) + mod_ref[0, 3:4, :]
        h_hi, h_lo = _split(h2)
        both = _mm(h_hi, wr_ref[...])
        logits = both[:, :LANES] + both[:, LANES:] + _mm(h_lo, wr_ref[:, :LANES]) + br_ref[...]
        bucket, w_lo, w_hi = _route(logits.T)
        h2_ref[rows, :d] = h2
        row = lax.broadcasted_iota(jnp.int32, (LANES, SUB_TILE), 0)
        wts = jnp.where(row == 0, w_lo, jnp.where(row == 1, w_hi, 0.0))
        h2_ref[rows, d:] = wts.T
        row8 = lax.broadcasted_iota(jnp.int32, (SUBLANES, SUB_TILE), 0)
        info_ref[:, rows] = jnp.where(row8 == 0, bucket, 0.0)


def _outproj(x, f_out, a_out, modl, n2, w_o, wr, br, batch, seq, latent):
    d = x.shape[1]
    tt = min(seq, 2 * SUB_TILE)
    nt = seq // tt
    t = batch * seq
    row = lambda b, i: (b * nt + i, 0)
    const = lambda b, i: (0, 0)
    mod_map = (lambda b, i: (b + 1, 0, 0)) if latent else (lambda b, i: (0, 0, 0))
    return pl.pallas_call(
        _outproj_kernel,
        grid=(batch, nt),
        in_specs=[
            pl.BlockSpec((tt, d), row),
            pl.BlockSpec((tt, F_WIDTH), row),
            pl.BlockSpec((N_KV_HEADS, tt, GROUP_WIDTH), lambda b, i: (0, b * nt + i, 0)),
            pl.BlockSpec((1, N_MOD, d), mod_map),
            pl.BlockSpec((1, d), const),
            pl.BlockSpec((F_WIDTH + Q_WIDTH, d), const),
            pl.BlockSpec((d, 2 * LANES), const),
            pl.BlockSpec((1, LANES), const),
        ],
        out_specs=[
            pl.BlockSpec((tt, d), row),
            pl.BlockSpec((tt, d + LANES), row),
            pl.BlockSpec((SUBLANES, tt), lambda b, i: (0, b * nt + i)),
        ],
        out_shape=[
            jax.ShapeDtypeStruct((t, d), F32),
            jax.ShapeDtypeStruct((t, d + LANES), F32),
            jax.ShapeDtypeStruct((SUBLANES, t), F32),
        ],
        compiler_params=_params("arbitrary", "arbitrary"),
        name="outproj_latent" if latent else "outproj_ctx",
    )(x, f_out, a_out, modl, n2, w_o, wr, br)


def _rank_kernel(b_ref, pos_ref, start_ref, *, n_chunks, ch):
    rows = lax.broadcasted_iota(jnp.int32, (BUCKET_ROWS, ch), 0).astype(F32)
    ones = jnp.ones((ch, LANES), BF16)
    upper = (lax.broadcasted_iota(jnp.int32, (ch, ch), 0) < lax.broadcasted_iota(jnp.int32, (ch, ch), 1)).astype(BF16)

    def onehot(i):
        off = pl.multiple_of(i * ch, ch)
        return (rows == b_ref[:, pl.ds(off, ch)]).astype(BF16)

    count = lax.fori_loop(0, n_chunks, lambda i, c: c + _mm(onehot(i), ones), jnp.zeros((BUCKET_ROWS, LANES), F32))
    n_tiles = jnp.floor((count + float(MOE_TILE - 1)) * (1.0 / MOE_TILE))
    lower = (lax.broadcasted_iota(jnp.int32, (BUCKET_ROWS, BUCKET_ROWS), 1)
             < lax.broadcasted_iota(jnp.int32, (BUCKET_ROWS, BUCKET_ROWS), 0)).astype(BF16)
    start_tiles = _mm(lower, n_tiles.astype(BF16))
    start_ref[...] = start_tiles
    start_rows = start_tiles * float(MOE_TILE)

    def body(i, carry):
        oh = onehot(i)
        prefix = _mm(oh, upper)
        base = jnp.concatenate([start_rows + carry] * (ch // LANES), axis=1)
        pos = jnp.sum(oh.astype(F32) * (prefix + base), axis=0, keepdims=True)
        off = pl.multiple_of(i * ch, ch)
        pos_ref[:, pl.ds(off, ch)] = pos.astype(jnp.int32)
        return carry + _mm(oh, ones)

    lax.fori_loop(0, n_chunks, body, jnp.zeros((BUCKET_ROWS, LANES), F32))


def _rank(bucket_row):
    t = bucket_row.shape[1]
    ch = RANK_CHUNK
    return pl.pallas_call(
        functools.partial(_rank_kernel, n_chunks=t // ch, ch=ch),
        out_shape=[jax.ShapeDtypeStruct((1, t), jnp.int32), jax.ShapeDtypeStruct((BUCKET_ROWS, LANES), F32)],
        compiler_params=pltpu.CompilerParams(vmem_limit_bytes=VMEM_LIMIT),
        name="bucket_rank",
    )(bucket_row)


def _dispatch_kernel(*refs, tile_starts):
    n_src = len(tile_starts) - 1
    zero_ref, pos_ref = refs[:2]
    src_refs = refs[2:2 + n_src]
    dst_ref, zeros_sc, stage_sc, sem, ssem, zsem = refs[2 + n_src:]
    i = pl.program_id(0)
    tt = pos_ref.shape[2]

    def row_block(ref, start, rows):
        return ref.at[pl.ds(pl.multiple_of(start, SUBLANES), rows)]

    def zero_copy(t):
        return pltpu.make_async_copy(zeros_sc, row_block(dst_ref, t * MOE_TILE, MOE_TILE), zsem)

    @pl.when(i == 0)
    def _():
        zeros_sc[...] = jnp.zeros(zeros_sc.shape, F32)
        for wait in (False, True):
            def tile(t, c, wait=wait):
                @pl.when(zero_ref[t] == 1)
                def _():
                    zero_copy(t).wait() if wait else zero_copy(t).start()
                return c
            lax.fori_loop(0, zero_ref.shape[0], tile, 0)

    slot = i % 2

    def stage(t, s, wait):
        for k in range(n_src):
            @pl.when((t >= tile_starts[k]) & (t < tile_starts[k + 1]))
            def _(src_ref=src_refs[k], first_tile=tile_starts[k]):
                cp = pltpu.make_async_copy(row_block(src_ref, (t - first_tile) * tt, tt), stage_sc.at[s], ssem.at[s])
                cp.wait() if wait else cp.start()

    def wait_rows(s):
        pltpu.make_async_copy(stage_sc.at[s], row_block(dst_ref, 0, tt), sem.at[s]).wait()

    @pl.when(i == 0)
    def _():
        stage(i, slot, wait=False)

    stage(i, slot, wait=True)

    def body(r, c):
        pltpu.make_async_copy(stage_sc.at[slot, pl.ds(r, 1)], dst_ref.at[pl.ds(pos_ref[0, 0, r], 1)],
                              sem.at[slot]).start()
        return c

    lax.fori_loop(0, tt, body, 0, unroll=8)

    @pl.when(i > 0)
    def _():
        wait_rows(1 - slot)

    @pl.when(i + 1 < pl.num_programs(0))
    def _():
        stage(i + 1, 1 - slot, wait=False)

    @pl.when(i == pl.num_programs(0) - 1)
    def _():
        wait_rows(slot)


def _dispatch(pos3, sources, zero_mask, n_rows):
    nt, _, tt = pos3.shape
    width = sources[0].shape[1]
    tile_starts = [0]
    for src in sources:
        tile_starts.append(tile_starts[-1] + src.shape[0] // tt)
    any_spec = pl.BlockSpec(memory_space=pl.ANY)
    grid_spec = pltpu.PrefetchScalarGridSpec(
        num_scalar_prefetch=1,
        grid=(nt,),
        in_specs=[pl.BlockSpec((1, 1, tt), lambda i, z: (i, 0, 0), memory_space=pltpu.SMEM)] + [any_spec] * len(sources),
        out_specs=any_spec,
        scratch_shapes=[pltpu.VMEM((MOE_TILE, width), F32), pltpu.VMEM((2, tt, width), F32),
                        pltpu.SemaphoreType.DMA((2,)), pltpu.SemaphoreType.DMA((2,)), pltpu.SemaphoreType.DMA(())],
    )
    return pl.pallas_call(
        functools.partial(_dispatch_kernel, tile_starts=tuple(tile_starts)),
        grid_spec=grid_spec,
        out_shape=jax.ShapeDtypeStruct((n_rows, width), F32),
        compiler_params=_params("arbitrary"),
        name="moe_dispatch",
    )(zero_mask, pos3, *sources)


def _moe_kernel(ea_ref, eb_ref, valid_ref, h_ref, wga, wua, wda, wgb, wub, wdb, o_ref):
    del ea_ref, eb_ref
    i = pl.program_id(0)
    d = o_ref.shape[1]

    @pl.when(valid_ref[i] == 1)
    def _():
        hb = h_ref[:, :d].astype(BF16)

        def expert(wg, wu, wd):
            hid = _silu(_mm(hb, wg[0])) * _mm(hb, wu[0])
            return _mm(hid.astype(BF16), wd[0])

        ya = expert(wga, wua, wda)
        yb = expert(wgb, wub, wdb)
        o_ref[...] = h_ref[:, d:d + 1] * ya + h_ref[:, d + 1:d + 2] * yb

    @pl.when(valid_ref[i] == 0)
    def _():
        o_ref[...] = jnp.zeros(o_ref.shape, F32)


def _moe(ea, eb, valid, h2s, wg, wu, wd):
    n_tiles = ea.shape[0]
    tm = MOE_TILE
    d = wd.shape[2]
    de = wd.shape[1]
    a_map = lambda i, ea, eb, valid: (ea[i], 0, 0)
    b_map = lambda i, ea, eb, valid: (eb[i], 0, 0)
    row = lambda i, ea, eb, valid: (i, 0)
    grid_spec = pltpu.PrefetchScalarGridSpec(
        num_scalar_prefetch=3,
        grid=(n_tiles,),
        in_specs=[
            pl.BlockSpec((tm, h2s.shape[1]), row),
            pl.BlockSpec((1, d, de), a_map), pl.BlockSpec((1, d, de), a_map), pl.BlockSpec((1, de, d), a_map),
            pl.BlockSpec((1, d, de), b_map), pl.BlockSpec((1, d, de), b_map), pl.BlockSpec((1, de, d), b_map),
        ],
        out_specs=pl.BlockSpec((tm, d), row),
    )
    return pl.pallas_call(
        _moe_kernel,
        grid_spec=grid_spec,
        out_shape=jax.ShapeDtypeStruct((n_tiles * tm, d), F32),
        compiler_params=_params("arbitrary"),
        name="moe_experts",
    )(ea, eb, valid, h2s, wg, wu, wd, wg, wu, wd)


def _combine_kernel(pos_ref, next_pos_ref, x_ref, mod_ref, ys_ref, o_ref, buf, sem):
    i = pl.program_id(0)
    tt = x_ref.shape[0]
    slot = i % 2

    def issue(p_ref, s):
        def body(r, c):
            pltpu.make_async_copy(ys_ref.at[pl.ds(p_ref[0, 0, r], 1)], buf.at[s, pl.ds(r, 1)], sem.at[s]).start()
            return c
        lax.fori_loop(0, tt, body, 0, unroll=8)

    @pl.when(i == 0)
    def _():
        issue(pos_ref, 0)

    @pl.when(i + 1 < pl.num_programs(0))
    def _():
        issue(next_pos_ref, 1 - slot)

    pltpu.make_async_copy(ys_ref.at[pl.ds(0, tt)], buf.at[slot], sem.at[slot]).wait()
    o_ref[...] = x_ref[...] + mod_ref[0, 5:6, :] * buf[slot]


def _combine(pos3, x1, modl, ys, batch, seq, latent):
    d = x1.shape[1]
    tt = TOKEN_TILE
    nt = seq // tt
    mod_map = (lambda i: (i // nt + 1, 0, 0)) if latent else (lambda i: (0, 0, 0))
    return pl.pallas_call(
        _combine_kernel,
        grid=(batch * nt,),
        in_specs=[
            pl.BlockSpec((1, 1, tt), lambda i: (i, 0, 0), memory_space=pltpu.SMEM),
            pl.BlockSpec((1, 1, tt), lambda i: (jnp.minimum(i + 1, batch * nt - 1), 0, 0), memory_space=pltpu.SMEM),
            pl.BlockSpec((tt, d), lambda i: (i, 0)),
            pl.BlockSpec((1, N_MOD, d), mod_map),
            pl.BlockSpec(memory_space=pl.ANY),
        ],
        out_specs=pl.BlockSpec((tt, d), lambda i: (i, 0)),
        out_shape=jax.ShapeDtypeStruct(x1.shape, F32),
        scratch_shapes=[pltpu.VMEM((2, tt, d), F32), pltpu.SemaphoreType.DMA((2,))],
        compiler_params=_params("arbitrary"),
        name="moe_combine_latent" if latent else "moe_combine_ctx",
    )(pos3, pos3, x1, modl, ys)


def _rope_tables(seq):
    rows = seq // GRID_W
    row = jnp.repeat(jnp.arange(rows, dtype=F32), GRID_W)
    col = jnp.tile(jnp.arange(GRID_W, dtype=F32), rows)
    inv = ROPE_THETA ** (-jnp.arange(0, ROPE_AXIS_DIM, 2, dtype=F32) / ROPE_AXIS_DIM)
    cos_parts, sin_parts = [], []
    for pos in (row, col):
        ang = pos[:, None] * inv
        cos_parts += [jnp.cos(ang), jnp.cos(ang)]
        sin_parts += [-jnp.sin(ang), jnp.sin(ang)]
    cos = jnp.concatenate(cos_parts, axis=-1)
    sin = jnp.concatenate(sin_parts, axis=-1)
    return jnp.tile(cos, (1, LANES // HEAD_DIM)), jnp.tile(sin, (1, LANES // HEAD_DIM))


def _tile_tables(start_tiles, n_tiles):
    st = start_tiles[:N_BUCKETS + 1, 0].astype(jnp.int32)
    tiles = jnp.arange(n_tiles, dtype=jnp.int32)
    total = st[N_BUCKETS]
    bucket = jnp.sum((jnp.minimum(tiles, total - 1)[:, None] >= st[None, 1:]).astype(jnp.int32), axis=1)
    bucket = jnp.minimum(bucket, N_BUCKETS - 1)
    ends_bucket = jnp.any((tiles + 1)[:, None] == st[None, 1:], axis=1)
    may_pad = ((tiles >= total) | ends_bucket).astype(jnp.int32)
    pairs = [(a, b) for a in range(EXPERTS_PER_GROUP) for b in range(a + 1, EXPERTS_PER_GROUP)]
    lo = jnp.asarray([p[0] for p in pairs], jnp.int32)
    hi = jnp.asarray([p[1] for p in pairs], jnp.int32)
    grp = bucket // PAIRS_PER_GROUP
    pair = bucket % PAIRS_PER_GROUP
    ea = grp * EXPERTS_PER_GROUP + lo[pair]
    eb = grp * EXPERTS_PER_GROUP + hi[pair]
    return ea, eb, (tiles < total).astype(jnp.int32), may_pad


def _ctx_kv(cache_k_l, cache_v_l):
    b, p = cache_k_l.shape[:2]
    ckt = cache_k_l.transpose(0, 2, 3, 1).astype(BF16)
    v = cache_v_l.transpose(0, 2, 1, 3)
    pad = jnp.zeros((b, N_KV_HEADS, p, LANES - HEAD_DIM), F32).at[..., 0].set(1.0)
    return ckt, jnp.concatenate([v, pad], axis=-1).astype(BF16)


def kernel(x_prompt, x_sample, cache_k, cache_v, c, c_ctx, norm1_g, norm2_g, w_mod, b_mod, w_in, w_fourier,
           q_norm_g, k_norm_g, w_out, w_router_group, b_router_group, w_router_expert, b_router_expert,
           w_gate, w_up, w_down):
    batch, seq, d = x_prompt.shape
    dec_batch, dec_seq, _ = x_sample.shape
    depth = w_in.shape[0]
    streams = [
        dict(latent=False, batch=batch, seq=seq, x=x_prompt.reshape(batch * seq, d)),
        dict(latent=True, batch=dec_batch, seq=dec_seq, x=x_sample.reshape(dec_batch * dec_seq, d)),
    ]
    t_total = sum(s["batch"] * s["seq"] for s in streams)
    n_sorted_tiles = t_total // MOE_TILE + N_BUCKETS

    cond_rows = 2 * SUBLANES
    cs = jnp.zeros((cond_rows, d), F32).at[0].set(c_ctx).at[1:1 + dec_batch].set(c)
    mod = _adaln_mod(cs, w_mod, b_mod).reshape(depth, cond_rows, N_MOD, d)

    seg = np.arange(LANES) // HEAD_DIM
    same_head = (seg[:, None] == seg[None, :]).astype(np.float32) / HEAD_DIM
    bd = jnp.asarray(np.concatenate([same_head, same_head], axis=0), BF16)
    rope_tabs = _rope_tables(dec_seq)
    kc_layers, vc_layers = [], []

    for l in range(depth):
        modl = mod[l]
        w_in_l = w_in[l].astype(BF16)
        w_o_l = w_out[l].astype(BF16)
        qg = jnp.tile(q_norm_g[l], N_Q_HEADS)[None, :]
        kg = jnp.tile(k_norm_g[l], N_KV_HEADS)[None, :]
        n_router = N_EXPERT_GROUPS + N_EXPERTS
        wr = jnp.concatenate([w_router_group[l], w_router_expert[l], jnp.zeros((d, LANES - n_router), F32)], axis=1)
        wr = jnp.concatenate(_split(wr), axis=1)
        br = jnp.concatenate([b_router_group[l], b_router_expert[l], jnp.zeros((LANES - n_router,), F32)])[None, :]
        wg_l, wu_l, wd_l = w_gate[l].astype(BF16), w_up[l].astype(BF16), w_down[l].astype(BF16)

        for s in streams:
            b_, n_, lat = s["batch"], s["seq"], s["latent"]
            res = _inproj(s["x"], modl, norm1_g[l][None, :], w_in_l, qg, kg, bd, rope_tabs, b_, n_, lat)
            f_in, q, kt, vx = res[:4]
            if lat:
                ctx = _ctx_kv(cache_k[:, l], cache_v[:, l])
            else:
                ctx = None
                kc_layers.append(res[4])
                vc_layers.append(res[5])
            fourier = _fourier_fft if _uses_fft(n_) else _fourier_dense
            f_out = fourier(f_in, w_fourier[l].astype(BF16), b_, n_)
            a_out = _attention(q, kt, vx, ctx, b_, n_)
            s["x1"], s["h2e"], s["info"] = _outproj(s["x"], f_out, a_out, modl, norm2_g[l][None, :], w_o_l, wr, br,
                                                    b_, n_, lat)

        bucket_row = jnp.concatenate([s["info"][0:1] for s in streams], axis=1)
        pos, start_tiles = _rank(bucket_row)
        ea, eb, valid, may_pad = _tile_tables(start_tiles, n_sorted_tiles)

        pos3 = pos.reshape(t_total // TOKEN_TILE, 1, TOKEN_TILE)
        h2s = _dispatch(pos3, [s["h2e"] for s in streams], may_pad, n_sorted_tiles * MOE_TILE)
        off = 0
        for s in streams:
            n_tok_tiles = s["batch"] * s["seq"] // TOKEN_TILE
            s["pos3"] = pos3[off:off + n_tok_tiles]
            off += n_tok_tiles
        ys = _moe(ea, eb, valid, h2s, wg_l, wu_l, wd_l)
        for s in streams:
            s["x"] = _combine(s["pos3"], s["x1"], modl, ys, s["batch"], s["seq"], s["latent"])

    y_prompt = streams[0]["x"].reshape(batch, seq, d)
    y_sample = streams[1]["x"].reshape(dec_batch, dec_seq, d)
    cache_shape = (batch, depth, seq, N_KV_HEADS, HEAD_DIM)
    new_k = jnp.stack([k.reshape(batch, seq, KV_WIDTH) for k in kc_layers], axis=1).reshape(cache_shape)
    new_v = jnp.stack([v.reshape(batch, seq, KV_WIDTH) for v in vc_layers], axis=1).reshape(cache_shape)
    return (y_prompt, y_sample, new_k, new_v)
```

```python
import functools

import numpy as np
import jax
import jax.numpy as jnp
from jax import lax
from jax.experimental import pallas as pl
from jax.experimental.pallas import tpu as pltpu

F32 = jnp.float32
BF16 = jnp.bfloat16

HEAD_DIM = 64
N_Q_HEADS = 12
N_KV_HEADS = 4
KV_GROUP = N_Q_HEADS // N_KV_HEADS
Q_WIDTH = N_Q_HEADS * HEAD_DIM
GROUP_WIDTH = KV_GROUP * HEAD_DIM
KV_WIDTH = N_KV_HEADS * HEAD_DIM
F_GROUPS = 4
F_GROUP_DIM = 64
F_WIDTH = F_GROUPS * F_GROUP_DIM
GRID_W = 64
ROPE_AXIS_DIM = HEAD_DIM // 2
ROPE_HALF = ROPE_AXIS_DIM // 2
ROPE_THETA = 10000.0
N_EXPERT_GROUPS = 4
EXPERTS_PER_GROUP = 4
N_EXPERTS = N_EXPERT_GROUPS * EXPERTS_PER_GROUP
PAIRS_PER_GROUP = 6
N_BUCKETS = N_EXPERT_GROUPS * PAIRS_PER_GROUP
N_MOD = 6
EPS = 1e-6
ATTN_SCALE = HEAD_DIM ** -0.5
LOG2_E = 1.4426950408889634

LANES = 128
SUBLANES = 8

TOKEN_TILE = 256
SUB_TILE = 256
MOE_TILE = 256
ATTN_Q_TILE = 512
ATTN_K_TILE = 512
ATTN_HEADS_PER_STEP = 1
FFT_TOKEN_TILE = 512
RANK_CHUNK = 512
BUCKET_ROWS = 32
VMEM_LIMIT = 48 * 1024 * 1024
NEG_BIG = -1e30


def _split(x):
    hi = x.astype(BF16)
    lo = (x - hi.astype(F32)).astype(BF16)
    return hi, lo


def _mm(a, b):
    return jnp.dot(a, b, preferred_element_type=F32)


def _mm3(a, b):
    ah, al = _split(a)
    bh, bl = _split(b)
    return _mm(ah, bh) + (_mm(ah, bl) + _mm(al, bh))


def _silu(x):
    return x / (1.0 + jnp.exp(-x))


def _uses_fft(seq):
    return seq > TOKEN_TILE


def _params(*sem):
    return pltpu.CompilerParams(dimension_semantics=sem, vmem_limit_bytes=VMEM_LIMIT)


def _mod_kernel(c_ref, w_ref, b_ref, o_ref):
    o_ref[0] = _mm3(_silu(c_ref[...]), w_ref[0]) + b_ref[0]


def _adaln_mod(cs, w_mod, b_mod):
    n_layers, d, m = w_mod.shape
    r = cs.shape[0]
    tn = d
    return pl.pallas_call(
        _mod_kernel,
        grid=(n_layers, m // tn),
        in_specs=[
            pl.BlockSpec((r, d), lambda l, j: (0, 0)),
            pl.BlockSpec((1, d, tn), lambda l, j: (l, 0, j)),
            pl.BlockSpec((1, 1, tn), lambda l, j: (l, 0, j)),
        ],
        out_specs=pl.BlockSpec((1, r, tn), lambda l, j: (l, 0, j)),
        out_shape=jax.ShapeDtypeStruct((n_layers, r, m), F32),
        compiler_params=_params("arbitrary", "arbitrary"),
        name="adaln_mod",
    )(cs, w_mod, b_mod.reshape(n_layers, 1, m))


def _head_norm(z, gain, bd):
    outs = []
    for j in range(z.shape[1] // LANES):
        zj = z[:, LANES * j:LANES * (j + 1)]
        hi, lo = _split(zj * zj)
        msq = _mm(jnp.concatenate([hi, lo], axis=1), bd)
        outs.append(zj * lax.rsqrt(msq + EPS))
    return jnp.concatenate(outs, axis=1) * gain


def _rope(z, cos, sin):
    lane = lax.broadcasted_iota(jnp.int32, (z.shape[0], LANES), 1)
    first_half = (lane % ROPE_AXIS_DIM) < ROPE_HALF
    outs = []
    for j in range(z.shape[1] // LANES):
        zj = z[:, LANES * j:LANES * (j + 1)]
        partner = jnp.where(first_half, pltpu.roll(zj, LANES - ROPE_HALF, 1), pltpu.roll(zj, ROPE_HALF, 1))
        outs.append(zj * cos + partner * sin)
    return jnp.concatenate(outs, axis=1)


def _inproj_kernel(*refs, latent, grid_major):
    if latent:
        (x_ref, mod_ref, n1_ref, w_ref, qg_ref, kg_ref, bd_ref, cos_ref, sin_ref,
         f_ref, q_ref, kt_ref, vx_ref) = refs
    else:
        (x_ref, mod_ref, n1_ref, w_ref, qg_ref, kg_ref, bd_ref,
         f_ref, q_ref, kt_ref, vx_ref, kc_ref, vc_ref) = refs
    bd = bd_ref[...]
    for r0 in range(0, x_ref.shape[0], SUB_TILE):
        rows = slice(r0, r0 + SUB_TILE)
        x = x_ref[rows, :]
        h = x * lax.rsqrt(jnp.mean(x * x, axis=-1, keepdims=True) + EPS) * n1_ref[...]
        h = h * (1.0 + mod_ref[0, 1:2, :]) + mod_ref[0, 0:1, :]
        p = _mm(h.astype(BF16), w_ref[...])
        if grid_major:
            for a in range(SUB_TILE // GRID_W):
                f_ref[0, :, r0 // GRID_W + a, :] = p[GRID_W * a:GRID_W * (a + 1), :F_WIDTH]
        else:
            f_ref[rows, :] = p[:, :F_WIDTH]
        q = _head_norm(p[:, F_WIDTH:F_WIDTH + Q_WIDTH], qg_ref[...], bd)
        k = _head_norm(p[:, F_WIDTH + Q_WIDTH:F_WIDTH + Q_WIDTH + KV_WIDTH], kg_ref[...], bd)
        v = p[:, F_WIDTH + Q_WIDTH + KV_WIDTH:]
        if latent:
            cos = cos_ref[rows, :]
            sin = sin_ref[rows, :]
            q = _rope(q, cos, sin)
            k = _rope(k, cos, sin)
        else:
            kc_ref[rows, :] = k
            vc_ref[rows, :] = v
        qs = (q * (ATTN_SCALE * LOG2_E)).astype(BF16)
        for g in range(N_KV_HEADS):
            q_ref[g, rows, :] = qs[:, GROUP_WIDTH * g:GROUP_WIDTH * (g + 1)]
        kt_ref[0, :, :, rows] = k.T.reshape(N_KV_HEADS, HEAD_DIM, SUB_TILE).astype(BF16)
        lane = lax.broadcasted_iota(jnp.int32, (SUB_TILE, LANES), 1)
        ones_col = (lane == HEAD_DIM).astype(F32)
        for j in range(KV_WIDTH // LANES):
            vj = v[:, LANES * j:LANES * (j + 1)]
            vx_ref[0, 2 * j, rows, :] = jnp.where(lane < HEAD_DIM, vj, ones_col).astype(BF16)
            vx_ref[0, 2 * j + 1, rows, :] = jnp.where(lane < HEAD_DIM, pltpu.roll(vj, HEAD_DIM, 1),
                                                      ones_col).astype(BF16)


def _inproj(x, modl, n1, w_in, qg, kg, bd, rope_tabs, batch, seq, latent):
    d = x.shape[1]
    grid_major = _uses_fft(seq)
    tt = FFT_TOKEN_TILE if grid_major else TOKEN_TILE
    nt = seq // tt
    in_w = w_in.shape[1]
    row = lambda b, i: (b * nt + i, 0)
    const = lambda b, i: (0, 0)
    mod_map = (lambda b, i: (b + 1, 0, 0)) if latent else (lambda b, i: (0, 0, 0))
    in_specs = [
        pl.BlockSpec((tt, d), row),
        pl.BlockSpec((1, N_MOD, d), mod_map),
        pl.BlockSpec((1, d), const),
        pl.BlockSpec((d, in_w), const),
        pl.BlockSpec((1, Q_WIDTH), const),
        pl.BlockSpec((1, KV_WIDTH), const),
        pl.BlockSpec((2 * LANES, LANES), const),
    ]
    args = [x, modl, n1, w_in, qg, kg, bd]
    if latent:
        in_specs += [pl.BlockSpec((tt, LANES), lambda b, i: (i, 0))] * 2
        args += list(rope_tabs)
    t = batch * seq
    out_shape = [
        jax.ShapeDtypeStruct((batch, GRID_W, seq // GRID_W, F_WIDTH) if grid_major else (t, F_WIDTH), F32),
        jax.ShapeDtypeStruct((N_KV_HEADS, t, GROUP_WIDTH), BF16),
        jax.ShapeDtypeStruct((batch, N_KV_HEADS, HEAD_DIM, seq), BF16),
        jax.ShapeDtypeStruct((batch, N_KV_HEADS, seq, LANES), BF16),
    ]
    out_specs = [
        (pl.BlockSpec((1, GRID_W, tt // GRID_W, F_WIDTH), lambda b, i: (b, 0, i, 0)) if grid_major
         else pl.BlockSpec((tt, F_WIDTH), row)),
        pl.BlockSpec((N_KV_HEADS, tt, GROUP_WIDTH), lambda b, i: (0, b * nt + i, 0)),
        pl.BlockSpec((1, N_KV_HEADS, HEAD_DIM, tt), lambda b, i: (b, 0, 0, i)),
        pl.BlockSpec((1, N_KV_HEADS, tt, LANES), lambda b, i: (b, 0, i, 0)),
    ]
    if not latent:
        out_shape += [jax.ShapeDtypeStruct((t, KV_WIDTH), F32)] * 2
        out_specs += [pl.BlockSpec((tt, KV_WIDTH), row)] * 2
    return pl.pallas_call(
        functools.partial(_inproj_kernel, latent=latent, grid_major=grid_major),
        grid=(batch, nt),
        in_specs=in_specs,
        out_specs=out_specs,
        out_shape=out_shape,
        compiler_params=_params("arbitrary", "arbitrary"),
        name="inproj_latent" if latent else "inproj_ctx",
    )(*args)


def _dft_cos_sin(n):
    idx = np.arange(n, dtype=np.int64)
    ang = 2.0 * np.pi * ((idx[:, None] * idx[None, :]) % n).astype(np.float64) / n
    return np.cos(ang), np.sin(ang)


def _block_diag(m, reps):
    n = m.shape[0]
    out = np.zeros((n * reps, n * reps), m.dtype)
    for g in range(reps):
        out[g * n:(g + 1) * n, g * n:(g + 1) * n] = m
    return out


def _mxu_const(x):
    return jnp.asarray(x, F32).astype(BF16)


def _channel_dft():
    c, s = _dft_cos_sin(F_GROUP_DIM)
    return _mxu_const(_block_diag(c, F_GROUPS)), _mxu_const(_block_diag(s, F_GROUPS))


def _fourier_dense_kernel(z_ref, cn_ref, sn_ref, bdc_ref, bds_ref, wf_ref, o_ref, *, scale):
    z = z_ref[...].astype(BF16)
    zc = _mm(z, bdc_ref[...]).astype(BF16)
    zs = _mm(z, bds_ref[...]).astype(BF16)
    y = (_mm(cn_ref[...], zc) - _mm(sn_ref[...], zs)) * scale
    o_ref[...] = _mm(y.astype(BF16), wf_ref[...])


def _fourier_dense(f_in, w_f, batch, seq):
    cn, sn = _dft_cos_sin(seq)
    bdc, bds = _channel_dft()
    const = lambda b: (0, 0)
    return pl.pallas_call(
        functools.partial(_fourier_dense_kernel, scale=float((seq * F_GROUP_DIM) ** -0.5)),
        grid=(batch,),
        in_specs=[
            pl.BlockSpec((seq, F_WIDTH), lambda b: (b, 0)),
            pl.BlockSpec((seq, seq), const),
            pl.BlockSpec((seq, seq), const),
            pl.BlockSpec((F_WIDTH, F_WIDTH), const),
            pl.BlockSpec((F_WIDTH, F_WIDTH), const),
            pl.BlockSpec((F_WIDTH, F_WIDTH), const),
        ],
        out_specs=pl.BlockSpec((seq, F_WIDTH), lambda b: (b, 0)),
        out_shape=jax.ShapeDtypeStruct((batch * seq, F_WIDTH), F32),
        compiler_params=_params("arbitrary"),
        name="fourier_dense",
    )(f_in, _mxu_const(cn), _mxu_const(sn), bdc, bds, w_f)


def _fft_kernel(z_ref, cs_ref, g_ref, bdc_ref, bds_ref, wf_ref, o_ref, y_sc, *, scale, r, cb):
    def rows(b, carry):
        yb = _mm(cs_ref[...], z_ref[0, b].astype(BF16))
        y_sc[0, :, b, :] = yb[:r]
        y_sc[1, :, b, :] = yb[r:]
        return carry

    lax.fori_loop(0, GRID_W, rows, 0, unroll=8)

    def cols(j, carry):
        xr, xi = [], []
        for cc in range(cb):
            c = j * cb + cc
            ystack = jnp.concatenate([y_sc[0, c], y_sc[1, c]], axis=0).astype(BF16)
            xc = _mm(g_ref[c], ystack)
            xr.append(xc[:GRID_W])
            xi.append(xc[GRID_W:])
        xr = jnp.concatenate(xr, axis=0).astype(BF16)
        xi = jnp.concatenate(xi, axis=0).astype(BF16)
        o = (_mm(xr, bdc_ref[...]) + _mm(xi, bds_ref[...])) * scale
        o = _mm(o.astype(BF16), wf_ref[...])
        for cc in range(cb):
            o_ref[0, :, j * cb + cc, :] = o[GRID_W * cc:GRID_W * (cc + 1)]
        return carry

    lax.fori_loop(0, r // cb, cols, 0, unroll=2)


def _fourier_fft(f_in, w_f, batch, seq):
    r = seq // GRID_W
    cr, sr = _dft_cos_sin(r)
    cs = _mxu_const(np.concatenate([cr, -sr], axis=0))
    bb = np.arange(GRID_W, dtype=np.int64)
    dd = np.arange(GRID_W, dtype=np.int64)
    cc = np.arange(r, dtype=np.int64)
    ang = 2.0 * np.pi * ((bb[None, None, :] * (r * dd[None, :, None] + cc[:, None, None])) % seq) / seq
    gr, gi = np.cos(ang), -np.sin(ang)
    g = _mxu_const(np.concatenate([np.concatenate([gr, -gi], axis=2),
                                   np.concatenate([gi, gr], axis=2)], axis=1))
    bdc, bds = _channel_dft()
    cb = min(r, 8)
    const = lambda b: (0, 0)
    out = pl.pallas_call(
        functools.partial(_fft_kernel, scale=float((seq * F_GROUP_DIM) ** -0.5), r=r, cb=cb),
        grid=(batch,),
        in_specs=[
            pl.BlockSpec((1, GRID_W, r, F_WIDTH), lambda b: (b, 0, 0, 0)),
            pl.BlockSpec((2 * r, r), const),
            pl.BlockSpec((r, 2 * GRID_W, 2 * GRID_W), lambda b: (0, 0, 0)),
            pl.BlockSpec((F_WIDTH, F_WIDTH), const),
            pl.BlockSpec((F_WIDTH, F_WIDTH), const),
            pl.BlockSpec((F_WIDTH, F_WIDTH), const),
        ],
        out_specs=pl.BlockSpec((1, GRID_W, r, F_WIDTH), lambda b: (b, 0, 0, 0)),
        out_shape=jax.ShapeDtypeStruct((batch, GRID_W, r, F_WIDTH), F32),
        scratch_shapes=[pltpu.VMEM((2, r, GRID_W, F_WIDTH), F32)],
        compiler_params=_params("arbitrary"),
        name="fourier_fft",
    )(f_in, cs, g, bdc, bds, w_f)
    return out.reshape(batch * seq, F_WIDTH)


def _attn_kernel(*refs, tk, n_chunks, has_ctx):
    if has_ctx:
        q_ref, kt_ref, vx_ref, ckt_ref, cvx_ref, o_ref = refs
    else:
        q_ref, kt_ref, vx_ref, o_ref = refs
    n_heads, tq = q_ref.shape[:2]
    m_rows = KV_GROUP * tq
    qs = []
    for j in range(n_heads):
        q = q_ref[j]
        qs.append(jnp.concatenate([q[:, HEAD_DIM * h:HEAD_DIM * (h + 1)] for h in range(KV_GROUP)], axis=0))

    def step(qg, kt, vv, carry):
        m, acc = carry
        s = _mm(qg, kt)
        tile_max = functools.reduce(
            jnp.maximum, [s[:, LANES * j:LANES * (j + 1)] for j in range(s.shape[1] // LANES)])
        m_new = jnp.maximum(m, jnp.max(tile_max, axis=-1, keepdims=True))
        p = jnp.exp2(s - m_new).astype(BF16)
        return m_new, acc * jnp.exp2(m - m_new) + _mm(p, vv)

    carries = [(jnp.full((m_rows, 1), NEG_BIG, F32), jnp.zeros((m_rows, LANES), F32)) for _ in range(n_heads)]
    if has_ctx:
        carries = [step(qs[j], ckt_ref[0, j], cvx_ref[0, j], carries[j]) for j in range(n_heads)]
    for i in range(n_chunks):
        lo = tk * i
        carries = [step(qs[j], kt_ref[0, j, :, lo:lo + tk], vx_ref[0, j, lo:lo + tk, :], carries[j])
                   for j in range(n_heads)]
    for j in range(n_heads):
        acc = carries[j][1]
        o = acc[:, :HEAD_DIM] / acc[:, HEAD_DIM:HEAD_DIM + 1]
        o_ref[j] = jnp.concatenate([o[h * tq:(h + 1) * tq] for h in range(KV_GROUP)], axis=1).astype(BF16)


def _attention(q, kt, vx, ctx, batch, seq):
    tq = min(ATTN_Q_TILE, seq)
    tk = min(ATTN_K_TILE, seq)
    nq = seq // tq
    hs = ATTN_HEADS_PER_STEP
    q_spec = pl.BlockSpec((hs, tq, GROUP_WIDTH), lambda b, g, i: (g, b * nq + i, 0))
    in_specs = [
        q_spec,
        pl.BlockSpec((1, hs, HEAD_DIM, seq), lambda b, g, i: (b, g, 0, 0)),
        pl.BlockSpec((1, hs, seq, LANES), lambda b, g, i: (b, g, 0, 0)),
    ]
    args = [q, kt, vx]
    if ctx is not None:
        ckt, cvx = ctx
        past = ckt.shape[-1]
        in_specs += [
            pl.BlockSpec((1, hs, HEAD_DIM, past), lambda b, g, i: (b, g, 0, 0)),
            pl.BlockSpec((1, hs, past, LANES), lambda b, g, i: (b, g, 0, 0)),
        ]
        args += [ckt, cvx]
    return pl.pallas_call(
        functools.partial(_attn_kernel, tk=tk, n_chunks=seq // tk, has_ctx=ctx is not None),
        grid=(batch, N_KV_HEADS // hs, nq),
        in_specs=in_specs,
        out_specs=q_spec,
        out_shape=jax.ShapeDtypeStruct(q.shape, BF16),
        compiler_params=_params("arbitrary", "arbitrary", "arbitrary"),
        name="attention_latent" if ctx is not None else "attention_ctx",
    )(*args)


def _first_index(vals, target):
    idx = jnp.full(target.shape, float(len(vals) - 1), F32)
    for j in range(len(vals) - 2, -1, -1):
        idx = jnp.where(vals[j] == target, float(j), idx)
    return idx


def _route(lt):
    rows = [lt[i:i + 1, :] for i in range(N_EXPERT_GROUPS + N_EXPERTS)]
    gl = rows[:N_EXPERT_GROUPS]
    gmax = functools.reduce(jnp.maximum, gl)
    gidx = _first_index(gl, gmax)
    g_w = 1.0 / functools.reduce(lambda a, b: a + b, [jnp.exp(v - gmax) for v in gl])
    es = []
    for j in range(EXPERTS_PER_GROUP):
        sel = rows[N_EXPERT_GROUPS + (N_EXPERT_GROUPS - 1) * EXPERTS_PER_GROUP + j]
        for g in range(N_EXPERT_GROUPS - 2, -1, -1):
            sel = jnp.where(gidx == float(g), rows[N_EXPERT_GROUPS + g * EXPERTS_PER_GROUP + j], sel)
        es.append(sel)
    e1 = functools.reduce(jnp.maximum, es)
    i1 = _first_index(es, e1)
    rest = [jnp.where(i1 == float(j), -jnp.inf, es[j]) for j in range(EXPERTS_PER_GROUP)]
    e2 = functools.reduce(jnp.maximum, rest)
    i2 = _first_index(rest, e2)
    t = jnp.exp(e2 - e1)
    w1 = g_w / (1.0 + t)
    w2 = w1 * t
    lo = jnp.minimum(i1, i2)
    hi = jnp.maximum(i1, i2)
    w_lo = jnp.where(i1 < i2, w1, w2)
    w_hi = jnp.where(i1 < i2, w2, w1)
    pair = lo * (7.0 - lo) * 0.5 + (hi - lo - 1.0)
    return gidx * float(PAIRS_PER_GROUP) + pair, w_lo, w_hi


def _outproj_kernel(x_ref, f_ref, a_ref, mod_ref, n2_ref, wo_ref, wr_ref, br_ref, x1_ref, h2_ref, info_ref):
    d = x_ref.shape[1]
    for r0 in range(0, x_ref.shape[0], SUB_TILE):
        rows = slice(r0, r0 + SUB_TILE)
        mix = _mm(f_ref[rows, :].astype(BF16), wo_ref[:F_WIDTH, :])
        for g in range(N_KV_HEADS):
            lo = F_WIDTH + GROUP_WIDTH * g
            mix += _mm(a_ref[g, rows, :], wo_ref[lo:lo + GROUP_WIDTH, :])
        x1 = x_ref[rows, :] + mod_ref[0, 2:3, :] * mix
        x1_ref[rows, :] = x1
        h2 = x1 * lax.rsqrt(jnp.mean(x1 * x1, axis=-1, keepdims=True) + EPS) * n2_ref[...]
        h2 = h2 * (1.0 + mod_ref[0, 4:5, :]) + mod_ref[0, 3:4, :]
        h_hi, h_lo = _split(h2)
        both = _mm(h_hi, wr_ref[...])
        logits = both[:, :LANES] + both[:, LANES:] + _mm(h_lo, wr_ref[:, :LANES]) + br_ref[...]
        bucket, w_lo, w_hi = _route(logits.T)
        h2_ref[rows, :d] = h2
        row = lax.broadcasted_iota(jnp.int32, (LANES, SUB_TILE), 0)
        wts = jnp.where(row == 0, w_lo, jnp.where(row == 1, w_hi, 0.0))
        h2_ref[rows, d:] = wts.T
        row8 = lax.broadcasted_iota(jnp.int32, (SUBLANES, SUB_TILE), 0)
        info_ref[:, rows] = jnp.where(row8 == 0, bucket, 0.0)


def _outproj(x, f_out, a_out, modl, n2, w_o, wr, br, batch, seq, latent):
    d = x.shape[1]
    tt = min(seq, 2 * SUB_TILE)
    nt = seq // tt
    t = batch * seq
    row = lambda b, i: (b * nt + i, 0)
    const = lambda b, i: (0, 0)
    mod_map = (lambda b, i: (b + 1, 0, 0)) if latent else (lambda b, i: (0, 0, 0))
    return pl.pallas_call(
        _outproj_kernel,
        grid=(batch, nt),
        in_specs=[
            pl.BlockSpec((tt, d), row),
            pl.BlockSpec((tt, F_WIDTH), row),
            pl.BlockSpec((N_KV_HEADS, tt, GROUP_WIDTH), lambda b, i: (0, b * nt + i, 0)),
            pl.BlockSpec((1, N_MOD, d), mod_map),
            pl.BlockSpec((1, d), const),
            pl.BlockSpec((F_WIDTH + Q_WIDTH, d), const),
            pl.BlockSpec((d, 2 * LANES), const),
            pl.BlockSpec((1, LANES), const),
        ],
        out_specs=[
            pl.BlockSpec((tt, d), row),
            pl.BlockSpec((tt, d + LANES), row),
            pl.BlockSpec((SUBLANES, tt), lambda b, i: (0, b * nt + i)),
        ],
        out_shape=[
            jax.ShapeDtypeStruct((t, d), F32),
            jax.ShapeDtypeStruct((t, d + LANES), F32),
            jax.ShapeDtypeStruct((SUBLANES, t), F32),
        ],
        compiler_params=_params("arbitrary", "arbitrary"),
        name="outproj_latent" if latent else "outproj_ctx",
    )(x, f_out, a_out, modl, n2, w_o, wr, br)


def _rank_kernel(b_ref, pos_ref, start_ref, *, n_chunks, ch):
    rows = lax.broadcasted_iota(jnp.int32, (BUCKET_ROWS, ch), 0).astype(F32)
    ones = jnp.ones((ch, LANES), BF16)
    upper = (lax.broadcasted_iota(jnp.int32, (ch, ch), 0) < lax.broadcasted_iota(jnp.int32, (ch, ch), 1)).astype(BF16)

    def onehot(i):
        off = pl.multiple_of(i * ch, ch)
        return (rows == b_ref[:, pl.ds(off, ch)]).astype(BF16)

    count = lax.fori_loop(0, n_chunks, lambda i, c: c + _mm(onehot(i), ones), jnp.zeros((BUCKET_ROWS, LANES), F32))
    n_tiles = jnp.floor((count + float(MOE_TILE - 1)) * (1.0 / MOE_TILE))
    lower = (lax.broadcasted_iota(jnp.int32, (BUCKET_ROWS, BUCKET_ROWS), 1)
             < lax.broadcasted_iota(jnp.int32, (BUCKET_ROWS, BUCKET_ROWS), 0)).astype(BF16)
    start_tiles = _mm(lower, n_tiles.astype(BF16))
    start_ref[...] = start_tiles
    start_rows = start_tiles * float(MOE_TILE)

    def body(i, carry):
        oh = onehot(i)
        prefix = _mm(oh, upper)
        base = jnp.concatenate([start_rows + carry] * (ch // LANES), axis=1)
        pos = jnp.sum(oh.astype(F32) * (prefix + base), axis=0, keepdims=True)
        off = pl.multiple_of(i * ch, ch)
        pos_ref[:, pl.ds(off, ch)] = pos.astype(jnp.int32)
        return carry + _mm(oh, ones)

    lax.fori_loop(0, n_chunks, body, jnp.zeros((BUCKET_ROWS, LANES), F32))


def _rank(bucket_row):
    t = bucket_row.shape[1]
    ch = RANK_CHUNK
    return pl.pallas_call(
        functools.partial(_rank_kernel, n_chunks=t // ch, ch=ch),
        out_shape=[jax.ShapeDtypeStruct((1, t), jnp.int32), jax.ShapeDtypeStruct((BUCKET_ROWS, LANES), F32)],
        compiler_params=pltpu.CompilerParams(vmem_limit_bytes=VMEM_LIMIT),
        name="bucket_rank",
    )(bucket_row)


def _dispatch_kernel(*refs, tile_starts):
    n_src = len(tile_starts) - 1
    zero_ref, pos_ref = refs[:2]
    src_refs = refs[2:2 + n_src]
    dst_ref, zeros_sc, stage_sc, sem, ssem, zsem = refs[2 + n_src:]
    i = pl.program_id(0)
    tt = pos_ref.shape[2]

    def row_block(ref, start, rows):
        return ref.at[pl.ds(pl.multiple_of(start, SUBLANES), rows)]

    def zero_copy(t):
        return pltpu.make_async_copy(zeros_sc, row_block(dst_ref, t * MOE_TILE, MOE_TILE), zsem)

    @pl.when(i == 0)
    def _():
        zeros_sc[...] = jnp.zeros(zeros_sc.shape, F32)
        for wait in (False, True):
            def tile(t, c, wait=wait):
                @pl.when(zero_ref[t] == 1)
                def _():
                    zero_copy(t).wait() if wait else zero_copy(t).start()
                return c
            lax.fori_loop(0, zero_ref.shape[0], tile, 0)

    slot = i % 3
    nxt = (i + 1) % 3
    last = pl.num_programs(0) - 1

    def stage(t, s, wait):
        for k in range(n_src):
            @pl.when((t >= tile_starts[k]) & (t < tile_starts[k + 1]))
            def _(src_ref=src_refs[k], first_tile=tile_starts[k]):
                cp = pltpu.make_async_copy(row_block(src_ref, (t - first_tile) * tt, tt), stage_sc.at[s], ssem.at[s])
                cp.wait() if wait else cp.start()

    def wait_rows(s):
        pltpu.make_async_copy(stage_sc.at[s], row_block(dst_ref, 0, tt), sem.at[s]).wait()

    @pl.when(i == 0)
    def _():
        stage(i, slot, wait=False)

    @pl.when(i >= 2)
    def _():
        wait_rows(nxt)

    @pl.when(i < last)
    def _():
        stage(i + 1, nxt, wait=False)

    stage(i, slot, wait=True)

    def body(r, c):
        pltpu.make_async_copy(stage_sc.at[slot, pl.ds(r, 1)], dst_ref.at[pl.ds(pos_ref[0, 0, r], 1)],
                              sem.at[slot]).start()
        return c

    lax.fori_loop(0, tt, body, 0, unroll=8)

    @pl.when((i == last) & (i >= 1))
    def _():
        wait_rows((i + 2) % 3)

    @pl.when(i == last)
    def _():
        wait_rows(slot)


def _dispatch(pos3, sources, zero_mask, n_rows):
    nt, _, tt = pos3.shape
    width = sources[0].shape[1]
    tile_starts = [0]
    for src in sources:
        tile_starts.append(tile_starts[-1] + src.shape[0] // tt)
    any_spec = pl.BlockSpec(memory_space=pl.ANY)
    grid_spec = pltpu.PrefetchScalarGridSpec(
        num_scalar_prefetch=1,
        grid=(nt,),
        in_specs=[pl.BlockSpec((1, 1, tt), lambda i, z: (i, 0, 0), memory_space=pltpu.SMEM)] + [any_spec] * len(sources),
        out_specs=any_spec,
        scratch_shapes=[pltpu.VMEM((MOE_TILE, width), F32), pltpu.VMEM((3, tt, width), F32),
                        pltpu.SemaphoreType.DMA((3,)), pltpu.SemaphoreType.DMA((3,)), pltpu.SemaphoreType.DMA(())],
    )
    return pl.pallas_call(
        functools.partial(_dispatch_kernel, tile_starts=tuple(tile_starts)),
        grid_spec=grid_spec,
        out_shape=jax.ShapeDtypeStruct((n_rows, width), F32),
        compiler_params=_params("arbitrary"),
        name="moe_dispatch",
    )(zero_mask, pos3, *sources)


def _moe_kernel(ea_ref, eb_ref, valid_ref, h_ref, wga, wua, wda, wgb, wub, wdb, o_ref):
    del ea_ref, eb_ref
    i = pl.program_id(0)
    d = o_ref.shape[1]

    @pl.when(valid_ref[i] == 1)
    def _():
        hb = h_ref[:, :d].astype(BF16)

        def expert(wg, wu, wd):
            hid = _silu(_mm(hb, wg[0])) * _mm(hb, wu[0])
            return _mm(hid.astype(BF16), wd[0])

        ya = expert(wga, wua, wda)
        yb = expert(wgb, wub, wdb)
        o_ref[...] = h_ref[:, d:d + 1] * ya + h_ref[:, d + 1:d + 2] * yb

    @pl.when(valid_ref[i] == 0)
    def _():
        o_ref[...] = jnp.zeros(o_ref.shape, F32)


def _moe(ea, eb, valid, h2s, wg, wu, wd):
    n_tiles = ea.shape[0]
    tm = MOE_TILE
    d = wd.shape[2]
    de = wd.shape[1]
    a_map = lambda i, ea, eb, valid: (ea[i], 0, 0)
    b_map = lambda i, ea, eb, valid: (eb[i], 0, 0)
    row = lambda i, ea, eb, valid: (i, 0)
    grid_spec = pltpu.PrefetchScalarGridSpec(
        num_scalar_prefetch=3,
        grid=(n_tiles,),
        in_specs=[
            pl.BlockSpec((tm, h2s.shape[1]), row),
            pl.BlockSpec((1, d, de), a_map), pl.BlockSpec((1, d, de), a_map), pl.BlockSpec((1, de, d), a_map),
            pl.BlockSpec((1, d, de), b_map), pl.BlockSpec((1, d, de), b_map), pl.BlockSpec((1, de, d), b_map),
        ],
        out_specs=pl.BlockSpec((tm, d), row),
    )
    return pl.pallas_call(
        _moe_kernel,
        grid_spec=grid_spec,
        out_shape=jax.ShapeDtypeStruct((n_tiles * tm, d), F32),
        compiler_params=_params("arbitrary"),
        name="moe_experts",
    )(ea, eb, valid, h2s, wg, wu, wd, wg, wu, wd)


def _combine_kernel(pos_ref, next_pos_ref, x_ref, mod_ref, ys_ref, o_ref, buf, sem):
    i = pl.program_id(0)
    tt = x_ref.shape[0]
    slot = i % 2

    def issue(p_ref, s):
        def body(r, c):
            pltpu.make_async_copy(ys_ref.at[pl.ds(p_ref[0, 0, r], 1)], buf.at[s, pl.ds(r, 1)], sem.at[s]).start()
            return c
        lax.fori_loop(0, tt, body, 0, unroll=8)

    @pl.when(i == 0)
    def _():
        issue(pos_ref, 0)

    @pl.when(i + 1 < pl.num_programs(0))
    def _():
        issue(next_pos_ref, 1 - slot)

    pltpu.make_async_copy(ys_ref.at[pl.ds(0, tt)], buf.at[slot], sem.at[slot]).wait()
    o_ref[...] = x_ref[...] + mod_ref[0, 5:6, :] * buf[slot]


def _combine(pos3, x1, modl, ys, batch, seq, latent):
    d = x1.shape[1]
    tt = TOKEN_TILE
    nt = seq // tt
    mod_map = (lambda i: (i // nt + 1, 0, 0)) if latent else (lambda i: (0, 0, 0))
    return pl.pallas_call(
        _combine_kernel,
        grid=(batch * nt,),
        in_specs=[
            pl.BlockSpec((1, 1, tt), lambda i: (i, 0, 0), memory_space=pltpu.SMEM),
            pl.BlockSpec((1, 1, tt), lambda i: (jnp.minimum(i + 1, batch * nt - 1), 0, 0), memory_space=pltpu.SMEM),
            pl.BlockSpec((tt, d), lambda i: (i, 0)),
            pl.BlockSpec((1, N_MOD, d), mod_map),
            pl.BlockSpec(memory_space=pl.ANY),
        ],
        out_specs=pl.BlockSpec((tt, d), lambda i: (i, 0)),
        out_shape=jax.ShapeDtypeStruct(x1.shape, F32),
        scratch_shapes=[pltpu.VMEM((2, tt, d), F32), pltpu.SemaphoreType.DMA((2,))],
        compiler_params=_params("arbitrary"),
        name="moe_combine_latent" if latent else "moe_combine_ctx",
    )(pos3, pos3, x1, modl, ys)


def _rope_tables(seq):
    rows = seq // GRID_W
    row = jnp.repeat(jnp.arange(rows, dtype=F32), GRID_W)
    col = jnp.tile(jnp.arange(GRID_W, dtype=F32), rows)
    inv = ROPE_THETA ** (-jnp.arange(0, ROPE_AXIS_DIM, 2, dtype=F32) / ROPE_AXIS_DIM)
    cos_parts, sin_parts = [], []
    for pos in (row, col):
        ang = pos[:, None] * inv
        cos_parts += [jnp.cos(ang), jnp.cos(ang)]
        sin_parts += [-jnp.sin(ang), jnp.sin(ang)]
    cos = jnp.concatenate(cos_parts, axis=-1)
    sin = jnp.concatenate(sin_parts, axis=-1)
    return jnp.tile(cos, (1, LANES // HEAD_DIM)), jnp.tile(sin, (1, LANES // HEAD_DIM))


def _tile_tables(start_tiles, n_tiles):
    st = start_tiles[:N_BUCKETS + 1, 0].astype(jnp.int32)
    tiles = jnp.arange(n_tiles, dtype=jnp.int32)
    total = st[N_BUCKETS]
    bucket = jnp.sum((jnp.minimum(tiles, total - 1)[:, None] >= st[None, 1:]).astype(jnp.int32), axis=1)
    bucket = jnp.minimum(bucket, N_BUCKETS - 1)
    ends_bucket = jnp.any((tiles + 1)[:, None] == st[None, 1:], axis=1)
    may_pad = ((tiles >= total) | ends_bucket).astype(jnp.int32)
    pairs = [(a, b) for a in range(EXPERTS_PER_GROUP) for b in range(a + 1, EXPERTS_PER_GROUP)]
    lo = jnp.asarray([p[0] for p in pairs], jnp.int32)
    hi = jnp.asarray([p[1] for p in pairs], jnp.int32)
    grp = bucket // PAIRS_PER_GROUP
    pair = bucket % PAIRS_PER_GROUP
    ea = grp * EXPERTS_PER_GROUP + lo[pair]
    eb = grp * EXPERTS_PER_GROUP + hi[pair]
    return ea, eb, (tiles < total).astype(jnp.int32), may_pad


def _ctx_kv(cache_k_l, cache_v_l):
    b, p = cache_k_l.shape[:2]
    ckt = cache_k_l.transpose(0, 2, 3, 1).astype(BF16)
    v = cache_v_l.transpose(0, 2, 1, 3)
    pad = jnp.zeros((b, N_KV_HEADS, p, LANES - HEAD_DIM), F32).at[..., 0].set(1.0)
    return ckt, jnp.concatenate([v, pad], axis=-1).astype(BF16)


def kernel(x_prompt, x_sample, cache_k, cache_v, c, c_ctx, norm1_g, norm2_g, w_mod, b_mod, w_in, w_fourier,
           q_norm_g, k_norm_g, w_out, w_router_group, b_router_group, w_router_expert, b_router_expert,
           w_gate, w_up, w_down):
    batch, seq, d = x_prompt.shape
    dec_batch, dec_seq, _ = x_sample.shape
    depth = w_in.shape[0]
    streams = [
        dict(latent=False, batch=batch, seq=seq, x=x_prompt.reshape(batch * seq, d)),
        dict(latent=True, batch=dec_batch, seq=dec_seq, x=x_sample.reshape(dec_batch * dec_seq, d)),
    ]
    t_total = sum(s["batch"] * s["seq"] for s in streams)
    n_sorted_tiles = t_total // MOE_TILE + N_BUCKETS

    cond_rows = 2 * SUBLANES
    cs = jnp.zeros((cond_rows, d), F32).at[0].set(c_ctx).at[1:1 + dec_batch].set(c)
    mod = _adaln_mod(cs, w_mod, b_mod).reshape(depth, cond_rows, N_MOD, d)

    seg = np.arange(LANES) // HEAD_DIM
    same_head = (seg[:, None] == seg[None, :]).astype(np.float32) / HEAD_DIM
    bd = jnp.asarray(np.concatenate([same_head, same_head], axis=0), BF16)
    rope_tabs = _rope_tables(dec_seq)
    kc_layers, vc_layers = [], []

    for l in range(depth):
        modl = mod[l]
        w_in_l = w_in[l].astype(BF16)
        w_o_l = w_out[l].astype(BF16)
        qg = jnp.tile(q_norm_g[l], N_Q_HEADS)[None, :]
        kg = jnp.tile(k_norm_g[l], N_KV_HEADS)[None, :]
        n_router = N_EXPERT_GROUPS + N_EXPERTS
        wr = jnp.concatenate([w_router_group[l], w_router_expert[l], jnp.zeros((d, LANES - n_router), F32)], axis=1)
        wr = jnp.concatenate(_split(wr), axis=1)
        br = jnp.concatenate([b_router_group[l], b_router_expert[l], jnp.zeros((LANES - n_router,), F32)])[None, :]
        wg_l, wu_l, wd_l = w_gate[l].astype(BF16), w_up[l].astype(BF16), w_down[l].astype(BF16)

        for s in streams:
            b_, n_, lat = s["batch"], s["seq"], s["latent"]
            res = _inproj(s["x"], modl, norm1_g[l][None, :], w_in_l, qg, kg, bd, rope_tabs, b_, n_, lat)
            f_in, q, kt, vx = res[:4]
            if lat:
                ctx = _ctx_kv(cache_k[:, l], cache_v[:, l])
            else:
                ctx = None
                kc_layers.append(res[4])
                vc_layers.append(res[5])
            fourier = _fourier_fft if _uses_fft(n_) else _fourier_dense
            f_out = fourier(f_in, w_fourier[l].astype(BF16), b_, n_)
            a_out = _attention(q, kt, vx, ctx, b_, n_)
            s["x1"], s["h2e"], s["info"] = _outproj(s["x"], f_out, a_out, modl, norm2_g[l][None, :], w_o_l, wr, br,
                                                    b_, n_, lat)

        bucket_row = jnp.concatenate([s["info"][0:1] for s in streams], axis=1)
        pos, start_tiles = _rank(bucket_row)
        ea, eb, valid, may_pad = _tile_tables(start_tiles, n_sorted_tiles)

        pos3 = pos.reshape(t_total // TOKEN_TILE, 1, TOKEN_TILE)
        h2s = _dispatch(pos3, [s["h2e"] for s in streams], may_pad, n_sorted_tiles * MOE_TILE)
        off = 0
        for s in streams:
            n_tok_tiles = s["batch"] * s["seq"] // TOKEN_TILE
            s["pos3"] = pos3[off:off + n_tok_tiles]
            off += n_tok_tiles
        ys = _moe(ea, eb, valid, h2s, wg_l, wu_l, wd_l)
        for s in streams:
            s["x"] = _combine(s["pos3"], s["x1"], modl, ys, s["batch"], s["seq"], s["latent"])

    y_prompt = streams[0]["x"].reshape(batch, seq, d)
    y_sample = streams[1]["x"].reshape(dec_batch, dec_seq, d)
    cache_shape = (batch, depth, seq, N_KV_HEADS, HEAD_DIM)
    new_k = jnp.stack([k.reshape(batch, seq, KV_WIDTH) for k in kc_layers], axis=1).reshape(cache_shape)
    new_v = jnp.stack([v.reshape(batch, seq, KV_WIDTH) for v in vc_layers], axis=1).reshape(cache_shape)
    return (y_prompt, y_sample, new_k, new_v)
```

```python
import functools

import numpy as np
import jax
import jax.numpy as jnp
from jax import lax
from jax.experimental import pallas as pl
from jax.experimental.pallas import tpu as pltpu

F32 = jnp.float32
BF16 = jnp.bfloat16

HEAD_DIM = 64
N_Q_HEADS = 12
N_KV_HEADS = 4
KV_GROUP = N_Q_HEADS // N_KV_HEADS
Q_WIDTH = N_Q_HEADS * HEAD_DIM
GROUP_WIDTH = KV_GROUP * HEAD_DIM
KV_WIDTH = N_KV_HEADS * HEAD_DIM
F_GROUPS = 4
F_GROUP_DIM = 64
F_WIDTH = F_GROUPS * F_GROUP_DIM
GRID_W = 64
ROPE_AXIS_DIM = HEAD_DIM // 2
ROPE_HALF = ROPE_AXIS_DIM // 2
ROPE_THETA = 10000.0
N_EXPERT_GROUPS = 4
EXPERTS_PER_GROUP = 4
N_EXPERTS = N_EXPERT_GROUPS * EXPERTS_PER_GROUP
PAIRS_PER_GROUP = 6
N_BUCKETS = N_EXPERT_GROUPS * PAIRS_PER_GROUP
N_MOD = 6
EPS = 1e-6
ATTN_SCALE = HEAD_DIM ** -0.5
LOG2_E = 1.4426950408889634

LANES = 128
SUBLANES = 8

TOKEN_TILE = 256
SUB_TILE = 256
MOE_TILE = 256
ATTN_Q_TILE = 512
ATTN_K_TILE = 512
ATTN_HEADS_PER_STEP = 1
FFT_TOKEN_TILE = 512
RANK_CHUNK = 512
BUCKET_ROWS = 32
VMEM_LIMIT = 48 * 1024 * 1024
NEG_BIG = -1e30


def _split(x):
    hi = x.astype(BF16)
    lo = (x - hi.astype(F32)).astype(BF16)
    return hi, lo


def _mm(a, b):
    return jnp.dot(a, b, preferred_element_type=F32)


def _mm3(a, b):
    ah, al = _split(a)
    bh, bl = _split(b)
    return _mm(ah, bh) + (_mm(ah, bl) + _mm(al, bh))


def _silu(x):
    return x / (1.0 + jnp.exp(-x))


def _uses_fft(seq):
    return seq > TOKEN_TILE


def _params(*sem):
    return pltpu.CompilerParams(dimension_semantics=sem, vmem_limit_bytes=VMEM_LIMIT)


def _mod_kernel(c_ref, w_ref, b_ref, o_ref):
    o_ref[0] = _mm3(_silu(c_ref[...]), w_ref[0]) + b_ref[0]


def _adaln_mod(cs, w_mod, b_mod):
    n_layers, d, m = w_mod.shape
    r = cs.shape[0]
    tn = d
    return pl.pallas_call(
        _mod_kernel,
        grid=(n_layers, m // tn),
        in_specs=[
            pl.BlockSpec((r, d), lambda l, j: (0, 0)),
            pl.BlockSpec((1, d, tn), lambda l, j: (l, 0, j)),
            pl.BlockSpec((1, 1, tn), lambda l, j: (l, 0, j)),
        ],
        out_specs=pl.BlockSpec((1, r, tn), lambda l, j: (l, 0, j)),
        out_shape=jax.ShapeDtypeStruct((n_layers, r, m), F32),
        compiler_params=_params("arbitrary", "arbitrary"),
        name="adaln_mod",
    )(cs, w_mod, b_mod.reshape(n_layers, 1, m))


def _head_norm(z, gain, bd):
    outs = []
    for j in range(z.shape[1] // LANES):
        zj = z[:, LANES * j:LANES * (j + 1)]
        hi, lo = _split(zj * zj)
        msq = _mm(jnp.concatenate([hi, lo], axis=1), bd)
        outs.append(zj * lax.rsqrt(msq + EPS))
    return jnp.concatenate(outs, axis=1) * gain


def _rope(z, cos, sin):
    lane = lax.broadcasted_iota(jnp.int32, (z.shape[0], LANES), 1)
    first_half = (lane % ROPE_AXIS_DIM) < ROPE_HALF
    outs = []
    for j in range(z.shape[1] // LANES):
        zj = z[:, LANES * j:LANES * (j + 1)]
        partner = jnp.where(first_half, pltpu.roll(zj, LANES - ROPE_HALF, 1), pltpu.roll(zj, ROPE_HALF, 1))
        outs.append(zj * cos + partner * sin)
    return jnp.concatenate(outs, axis=1)


def _inproj_kernel(*refs, latent, grid_major):
    if latent:
        (x_ref, mod_ref, n1_ref, w_ref, qg_ref, kg_ref, bd_ref, cos_ref, sin_ref,
         f_ref, q_ref, kt_ref, vx_ref) = refs
    else:
        (x_ref, mod_ref, n1_ref, w_ref, qg_ref, kg_ref, bd_ref,
         f_ref, q_ref, kt_ref, vx_ref, kc_ref, vc_ref) = refs
    bd = bd_ref[...]
    for r0 in range(0, x_ref.shape[0], SUB_TILE):
        rows = slice(r0, r0 + SUB_TILE)
        x = x_ref[rows, :]
        h = x * lax.rsqrt(jnp.mean(x * x, axis=-1, keepdims=True) + EPS) * n1_ref[...]
        h = h * (1.0 + mod_ref[0, 1:2, :]) + mod_ref[0, 0:1, :]
        p = _mm(h.astype(BF16), w_ref[...])
        if grid_major:
            for a in range(SUB_TILE // GRID_W):
                f_ref[0, :, r0 // GRID_W + a, :] = p[GRID_W * a:GRID_W * (a + 1), :F_WIDTH]
        else:
            f_ref[rows, :] = p[:, :F_WIDTH]
        q = _head_norm(p[:, F_WIDTH:F_WIDTH + Q_WIDTH], qg_ref[...], bd)
        k = _head_norm(p[:, F_WIDTH + Q_WIDTH:F_WIDTH + Q_WIDTH + KV_WIDTH], kg_ref[...], bd)
        v = p[:, F_WIDTH + Q_WIDTH + KV_WIDTH:]
        if latent:
            cos = cos_ref[rows, :]
            sin = sin_ref[rows, :]
            q = _rope(q, cos, sin)
            k = _rope(k, cos, sin)
        else:
            kc_ref[rows, :] = k
            vc_ref[rows, :] = v
        qs = (q * (ATTN_SCALE * LOG2_E)).astype(BF16)
        for g in range(N_KV_HEADS):
            q_ref[g, rows, :] = qs[:, GROUP_WIDTH * g:GROUP_WIDTH * (g + 1)]
        kt_ref[0, :, :, rows] = k.T.reshape(N_KV_HEADS, HEAD_DIM, SUB_TILE).astype(BF16)
        lane = lax.broadcasted_iota(jnp.int32, (SUB_TILE, LANES), 1)
        ones_col = (lane == HEAD_DIM).astype(F32)
        for j in range(KV_WIDTH // LANES):
            vj = v[:, LANES * j:LANES * (j + 1)]
            vx_ref[0, 2 * j, rows, :] = jnp.where(lane < HEAD_DIM, vj, ones_col).astype(BF16)
            vx_ref[0, 2 * j + 1, rows, :] = jnp.where(lane < HEAD_DIM, pltpu.roll(vj, HEAD_DIM, 1),
                                                      ones_col).astype(BF16)


def _inproj(x, modl, n1, w_in, qg, kg, bd, rope_tabs, batch, seq, latent):
    d = x.shape[1]
    grid_major = _uses_fft(seq)
    tt = FFT_TOKEN_TILE if grid_major else TOKEN_TILE
    nt = seq // tt
    in_w = w_in.shape[1]
    row = lambda b, i: (b * nt + i, 0)
    const = lambda b, i: (0, 0)
    mod_map = (lambda b, i: (b + 1, 0, 0)) if latent else (lambda b, i: (0, 0, 0))
    in_specs = [
        pl.BlockSpec((tt, d), row),
        pl.BlockSpec((1, N_MOD, d), mod_map),
        pl.BlockSpec((1, d), const),
        pl.BlockSpec((d, in_w), const),
        pl.BlockSpec((1, Q_WIDTH), const),
        pl.BlockSpec((1, KV_WIDTH), const),
        pl.BlockSpec((2 * LANES, LANES), const),
    ]
    args = [x, modl, n1, w_in, qg, kg, bd]
    if latent:
        in_specs += [pl.BlockSpec((tt, LANES), lambda b, i: (i, 0))] * 2
        args += list(rope_tabs)
    t = batch * seq
    out_shape = [
        jax.ShapeDtypeStruct((batch, GRID_W, seq // GRID_W, F_WIDTH) if grid_major else (t, F_WIDTH), F32),
        jax.ShapeDtypeStruct((N_KV_HEADS, t, GROUP_WIDTH), BF16),
        jax.ShapeDtypeStruct((batch, N_KV_HEADS, HEAD_DIM, seq), BF16),
        jax.ShapeDtypeStruct((batch, N_KV_HEADS, seq, LANES), BF16),
    ]
    out_specs = [
        (pl.BlockSpec((1, GRID_W, tt // GRID_W, F_WIDTH), lambda b, i: (b, 0, i, 0)) if grid_major
         else pl.BlockSpec((tt, F_WIDTH), row)),
        pl.BlockSpec((N_KV_HEADS, tt, GROUP_WIDTH), lambda b, i: (0, b * nt + i, 0)),
        pl.BlockSpec((1, N_KV_HEADS, HEAD_DIM, tt), lambda b, i: (b, 0, 0, i)),
        pl.BlockSpec((1, N_KV_HEADS, tt, LANES), lambda b, i: (b, 0, i, 0)),
    ]
    if not latent:
        out_shape += [jax.ShapeDtypeStruct((t, KV_WIDTH), F32)] * 2
        out_specs += [pl.BlockSpec((tt, KV_WIDTH), row)] * 2
    return pl.pallas_call(
        functools.partial(_inproj_kernel, latent=latent, grid_major=grid_major),
        grid=(batch, nt),
        in_specs=in_specs,
        out_specs=out_specs,
        out_shape=out_shape,
        compiler_params=_params("arbitrary", "arbitrary"),
        name="inproj_latent" if latent else "inproj_ctx",
    )(*args)


def _dft_cos_sin(n):
    idx = np.arange(n, dtype=np.int64)
    ang = 2.0 * np.pi * ((idx[:, None] * idx[None, :]) % n).astype(np.float64) / n
    return np.cos(ang), np.sin(ang)


def _block_diag(m, reps):
    n = m.shape[0]
    out = np.zeros((n * reps, n * reps), m.dtype)
    for g in range(reps):
        out[g * n:(g + 1) * n, g * n:(g + 1) * n] = m
    return out


def _mxu_const(x):
    return jnp.asarray(x, F32).astype(BF16)


def _channel_dft():
    c, s = _dft_cos_sin(F_GROUP_DIM)
    return _mxu_const(_block_diag(c, F_GROUPS)), _mxu_const(_block_diag(s, F_GROUPS))


def _fourier_dense_kernel(z_ref, cn_ref, sn_ref, bdc_ref, bds_ref, wf_ref, o_ref, *, scale):
    z = z_ref[...].astype(BF16)
    zc = _mm(z, bdc_ref[...]).astype(BF16)
    zs = _mm(z, bds_ref[...]).astype(BF16)
    y = (_mm(cn_ref[...], zc) - _mm(sn_ref[...], zs)) * scale
    o_ref[...] = _mm(y.astype(BF16), wf_ref[...])


def _fourier_dense(f_in, w_f, batch, seq):
    cn, sn = _dft_cos_sin(seq)
    bdc, bds = _channel_dft()
    const = lambda b: (0, 0)
    return pl.pallas_call(
        functools.partial(_fourier_dense_kernel, scale=float((seq * F_GROUP_DIM) ** -0.5)),
        grid=(batch,),
        in_specs=[
            pl.BlockSpec((seq, F_WIDTH), lambda b: (b, 0)),
            pl.BlockSpec((seq, seq), const),
            pl.BlockSpec((seq, seq), const),
            pl.BlockSpec((F_WIDTH, F_WIDTH), const),
            pl.BlockSpec((F_WIDTH, F_WIDTH), const),
            pl.BlockSpec((F_WIDTH, F_WIDTH), const),
        ],
        out_specs=pl.BlockSpec((seq, F_WIDTH), lambda b: (b, 0)),
        out_shape=jax.ShapeDtypeStruct((batch * seq, F_WIDTH), F32),
        compiler_params=_params("arbitrary"),
        name="fourier_dense",
    )(f_in, _mxu_const(cn), _mxu_const(sn), bdc, bds, w_f)


def _fft_kernel(z_ref, cs_ref, g_ref, bdc_ref, bds_ref, wf_ref, o_ref, y_sc, *, scale, r, cb):
    def rows(b, carry):
        yb = _mm(cs_ref[...], z_ref[0, b].astype(BF16))
        y_sc[0, :, b, :] = yb[:r]
        y_sc[1, :, b, :] = yb[r:]
        return carry

    lax.fori_loop(0, GRID_W, rows, 0, unroll=8)

    def cols(j, carry):
        xr, xi = [], []
        for cc in range(cb):
            c = j * cb + cc
            ystack = jnp.concatenate([y_sc[0, c], y_sc[1, c]], axis=0).astype(BF16)
            xc = _mm(g_ref[c], ystack)
            xr.append(xc[:GRID_W])
            xi.append(xc[GRID_W:])
        xr = jnp.concatenate(xr, axis=0).astype(BF16)
        xi = jnp.concatenate(xi, axis=0).astype(BF16)
        o = (_mm(xr, bdc_ref[...]) + _mm(xi, bds_ref[...])) * scale
        o = _mm(o.astype(BF16), wf_ref[...])
        for cc in range(cb):
            o_ref[0, :, j * cb + cc, :] = o[GRID_W * cc:GRID_W * (cc + 1)]
        return carry

    lax.fori_loop(0, r // cb, cols, 0, unroll=2)


def _fourier_fft(f_in, w_f, batch, seq):
    r = seq // GRID_W
    cr, sr = _dft_cos_sin(r)
    cs = _mxu_const(np.concatenate([cr, -sr], axis=0))
    bb = np.arange(GRID_W, dtype=np.int64)
    dd = np.arange(GRID_W, dtype=np.int64)
    cc = np.arange(r, dtype=np.int64)
    ang = 2.0 * np.pi * ((bb[None, None, :] * (r * dd[None, :, None] + cc[:, None, None])) % seq) / seq
    gr, gi = np.cos(ang), -np.sin(ang)
    g = _mxu_const(np.concatenate([np.concatenate([gr, -gi], axis=2),
                                   np.concatenate([gi, gr], axis=2)], axis=1))
    bdc, bds = _channel_dft()
    cb = min(r, 8)
    const = lambda b: (0, 0)
    out = pl.pallas_call(
        functools.partial(_fft_kernel, scale=float((seq * F_GROUP_DIM) ** -0.5), r=r, cb=cb),
        grid=(batch,),
        in_specs=[
            pl.BlockSpec((1, GRID_W, r, F_WIDTH), lambda b: (b, 0, 0, 0)),
            pl.BlockSpec((2 * r, r), const),
            pl.BlockSpec((r, 2 * GRID_W, 2 * GRID_W), lambda b: (0, 0, 0)),
            pl.BlockSpec((F_WIDTH, F_WIDTH), const),
            pl.BlockSpec((F_WIDTH, F_WIDTH), const),
            pl.BlockSpec((F_WIDTH, F_WIDTH), const),
        ],
        out_specs=pl.BlockSpec((1, GRID_W, r, F_WIDTH), lambda b: (b, 0, 0, 0)),
        out_shape=jax.ShapeDtypeStruct((batch, GRID_W, r, F_WIDTH), F32),
        scratch_shapes=[pltpu.VMEM((2, r, GRID_W, F_WIDTH), F32)],
        compiler_params=_params("arbitrary"),
        name="fourier_fft",
    )(f_in, cs, g, bdc, bds, w_f)
    return out.reshape(batch * seq, F_WIDTH)


def _attn_kernel(*refs, tk, n_chunks, has_ctx):
    if has_ctx:
        q_ref, kt_ref, vx_ref, ckt_ref, cvx_ref, o_ref = refs
    else:
        q_ref, kt_ref, vx_ref, o_ref = refs
    n_heads, tq = q_ref.shape[:2]
    m_rows = KV_GROUP * tq
    qs = []
    for j in range(n_heads):
        q = q_ref[j]
        qs.append(jnp.concatenate([q[:, HEAD_DIM * h:HEAD_DIM * (h + 1)] for h in range(KV_GROUP)], axis=0))

    def step(qg, kt, vv, carry):
        m, acc = carry
        s = _mm(qg, kt)
        tile_max = functools.reduce(
            jnp.maximum, [s[:, LANES * j:LANES * (j + 1)] for j in range(s.shape[1] // LANES)])
        m_new = jnp.maximum(m, jnp.max(tile_max, axis=-1, keepdims=True))
        p = jnp.exp2(s - m_new).astype(BF16)
        return m_new, acc * jnp.exp2(m - m_new) + _mm(p, vv)

    carries = [(jnp.full((m_rows, 1), NEG_BIG, F32), jnp.zeros((m_rows, LANES), F32)) for _ in range(n_heads)]
    if has_ctx:
        carries = [step(qs[j], ckt_ref[0, j], cvx_ref[0, j], carries[j]) for j in range(n_heads)]
    for i in range(n_chunks):
        lo = tk * i
        carries = [step(qs[j], kt_ref[0, j, :, lo:lo + tk], vx_ref[0, j, lo:lo + tk, :], carries[j])
                   for j in range(n_heads)]
    for j in range(n_heads):
        acc = carries[j][1]
        o = acc[:, :HEAD_DIM] / acc[:, HEAD_DIM:HEAD_DIM + 1]
        o_ref[j] = jnp.concatenate([o[h * tq:(h + 1) * tq] for h in range(KV_GROUP)], axis=1).astype(BF16)


def _attention(q, kt, vx, ctx, batch, seq):
    tq = min(ATTN_Q_TILE, seq)
    tk = min(ATTN_K_TILE, seq)
    nq = seq // tq
    hs = ATTN_HEADS_PER_STEP
    q_spec = pl.BlockSpec((hs, tq, GROUP_WIDTH), lambda b, g, i: (g, b * nq + i, 0))
    in_specs = [
        q_spec,
        pl.BlockSpec((1, hs, HEAD_DIM, seq), lambda b, g, i: (b, g, 0, 0)),
        pl.BlockSpec((1, hs, seq, LANES), lambda b, g, i: (b, g, 0, 0)),
    ]
    args = [q, kt, vx]
    if ctx is not None:
        ckt, cvx = ctx
        past = ckt.shape[-1]
        in_specs += [
            pl.BlockSpec((1, hs, HEAD_DIM, past), lambda b, g, i: (b, g, 0, 0)),
            pl.BlockSpec((1, hs, past, LANES), lambda b, g, i: (b, g, 0, 0)),
        ]
        args += [ckt, cvx]
    return pl.pallas_call(
        functools.partial(_attn_kernel, tk=tk, n_chunks=seq // tk, has_ctx=ctx is not None),
        grid=(batch, N_KV_HEADS // hs, nq),
        in_specs=in_specs,
        out_specs=q_spec,
        out_shape=jax.ShapeDtypeStruct(q.shape, BF16),
        compiler_params=_params("arbitrary", "arbitrary", "arbitrary"),
        name="attention_latent" if ctx is not None else "attention_ctx",
    )(*args)


def _first_index(vals, target):
    idx = jnp.full(target.shape, float(len(vals) - 1), F32)
    for j in range(len(vals) - 2, -1, -1):
        idx = jnp.where(vals[j] == target, float(j), idx)
    return idx


def _route(lt):
    rows = [lt[i:i + 1, :] for i in range(N_EXPERT_GROUPS + N_EXPERTS)]
    gl = rows[:N_EXPERT_GROUPS]
    gmax = functools.reduce(jnp.maximum, gl)
    gidx = _first_index(gl, gmax)
    g_w = 1.0 / functools.reduce(lambda a, b: a + b, [jnp.exp(v - gmax) for v in gl])
    es = []
    for j in range(EXPERTS_PER_GROUP):
        sel = rows[N_EXPERT_GROUPS + (N_EXPERT_GROUPS - 1) * EXPERTS_PER_GROUP + j]
        for g in range(N_EXPERT_GROUPS - 2, -1, -1):
            sel = jnp.where(gidx == float(g), rows[N_EXPERT_GROUPS + g * EXPERTS_PER_GROUP + j], sel)
        es.append(sel)
    e1 = functools.reduce(jnp.maximum, es)
    i1 = _first_index(es, e1)
    rest = [jnp.where(i1 == float(j), -jnp.inf, es[j]) for j in range(EXPERTS_PER_GROUP)]
    e2 = functools.reduce(jnp.maximum, rest)
    i2 = _first_index(rest, e2)
    t = jnp.exp(e2 - e1)
    w1 = g_w / (1.0 + t)
    w2 = w1 * t
    lo = jnp.minimum(i1, i2)
    hi = jnp.maximum(i1, i2)
    w_lo = jnp.where(i1 < i2, w1, w2)
    w_hi = jnp.where(i1 < i2, w2, w1)
    pair = lo * (7.0 - lo) * 0.5 + (hi - lo - 1.0)
    return gidx * float(PAIRS_PER_GROUP) + pair, w_lo, w_hi


def _outproj_kernel(x_ref, f_ref, a_ref, mod_ref, n2_ref, wo_ref, wr_ref, br_ref, x1_ref, h2_ref, info_ref):
    d = x_ref.shape[1]
    for r0 in range(0, x_ref.shape[0], SUB_TILE):
        rows = slice(r0, r0 + SUB_TILE)
        mix = _mm(f_ref[rows, :].astype(BF16), wo_ref[:F_WIDTH, :])
        for g in range(N_KV_HEADS):
            lo = F_WIDTH + GROUP_WIDTH * g
            mix += _mm(a_ref[g, rows, :], wo_ref[lo:lo + GROUP_WIDTH, :])
        x1 = x_ref[rows, :] + mod_ref[0, 2:3, :] * mix
        x1_ref[rows, :] = x1
        h2 = x1 * lax.rsqrt(jnp.mean(x1 * x1, axis=-1, keepdims=True) + EPS) * n2_ref[...]
        h2 = h2 * (1.0 + mod_ref[0, 4:5, :]) + mod_ref[0, 3:4, :]
        h_hi, h_lo = _split(h2)
        both = _mm(h_hi, wr_ref[...])
        logits = both[:, :LANES] + both[:, LANES:] + _mm(h_lo, wr_ref[:, :LANES]) + br_ref[...]
        bucket, w_lo, w_hi = _route(logits.T)
        h2_ref[rows, :d] = h2
        row = lax.broadcasted_iota(jnp.int32, (LANES, SUB_TILE), 0)
        wts = jnp.where(row == 0, w_lo, jnp.where(row == 1, w_hi, 0.0))
        h2_ref[rows, d:] = wts.T
        row8 = lax.broadcasted_iota(jnp.int32, (SUBLANES, SUB_TILE), 0)
        info_ref[:, rows] = jnp.where(row8 == 0, bucket, 0.0)


def _outproj(x, f_out, a_out, modl, n2, w_o, wr, br, batch, seq, latent):
    d = x.shape[1]
    tt = min(seq, 2 * SUB_TILE)
    nt = seq // tt
    t = batch * seq
    row = lambda b, i: (b * nt + i, 0)
    const = lambda b, i: (0, 0)
    mod_map = (lambda b, i: (b + 1, 0, 0)) if latent else (lambda b, i: (0, 0, 0))
    return pl.pallas_call(
        _outproj_kernel,
        grid=(batch, nt),
        in_specs=[
            pl.BlockSpec((tt, d), row),
            pl.BlockSpec((tt, F_WIDTH), row),
            pl.BlockSpec((N_KV_HEADS, tt, GROUP_WIDTH), lambda b, i: (0, b * nt + i, 0)),
            pl.BlockSpec((1, N_MOD, d), mod_map),
            pl.BlockSpec((1, d), const),
            pl.BlockSpec((F_WIDTH + Q_WIDTH, d), const),
            pl.BlockSpec((d, 2 * LANES), const),
            pl.BlockSpec((1, LANES), const),
        ],
        out_specs=[
            pl.BlockSpec((tt, d), row),
            pl.BlockSpec((tt, d + LANES), row),
            pl.BlockSpec((SUBLANES, tt), lambda b, i: (0, b * nt + i)),
        ],
        out_shape=[
            jax.ShapeDtypeStruct((t, d), F32),
            jax.ShapeDtypeStruct((t, d + LANES), F32),
            jax.ShapeDtypeStruct((SUBLANES, t), F32),
        ],
        compiler_params=_params("arbitrary", "arbitrary"),
        name="outproj_latent" if latent else "outproj_ctx",
    )(x, f_out, a_out, modl, n2, w_o, wr, br)


def _rank_kernel(b_ref, pos_ref, start_ref, *, n_chunks, ch):
    rows = lax.broadcasted_iota(jnp.int32, (BUCKET_ROWS, ch), 0).astype(F32)
    ones = jnp.ones((ch, LANES), BF16)
    upper = (lax.broadcasted_iota(jnp.int32, (ch, ch), 0) < lax.broadcasted_iota(jnp.int32, (ch, ch), 1)).astype(BF16)

    def onehot(i):
        off = pl.multiple_of(i * ch, ch)
        return (rows == b_ref[:, pl.ds(off, ch)]).astype(BF16)

    count = lax.fori_loop(0, n_chunks, lambda i, c: c + _mm(onehot(i), ones), jnp.zeros((BUCKET_ROWS, LANES), F32))
    n_tiles = jnp.floor((count + float(MOE_TILE - 1)) * (1.0 / MOE_TILE))
    lower = (lax.broadcasted_iota(jnp.int32, (BUCKET_ROWS, BUCKET_ROWS), 1)
             < lax.broadcasted_iota(jnp.int32, (BUCKET_ROWS, BUCKET_ROWS), 0)).astype(BF16)
    start_tiles = _mm(lower, n_tiles.astype(BF16))
    start_ref[...] = start_tiles
    start_rows = start_tiles * float(MOE_TILE)

    def body(i, carry):
        oh = onehot(i)
        prefix = _mm(oh, upper)
        base = jnp.concatenate([start_rows + carry] * (ch // LANES), axis=1)
        pos = jnp.sum(oh.astype(F32) * (prefix + base), axis=0, keepdims=True)
        off = pl.multiple_of(i * ch, ch)
        pos_ref[:, pl.ds(off, ch)] = pos.astype(jnp.int32)
        return carry + _mm(oh, ones)

    lax.fori_loop(0, n_chunks, body, jnp.zeros((BUCKET_ROWS, LANES), F32))


def _rank(bucket_row):
    t = bucket_row.shape[1]
    ch = RANK_CHUNK
    return pl.pallas_call(
        functools.partial(_rank_kernel, n_chunks=t // ch, ch=ch),
        out_shape=[jax.ShapeDtypeStruct((1, t), jnp.int32), jax.ShapeDtypeStruct((BUCKET_ROWS, LANES), F32)],
        compiler_params=pltpu.CompilerParams(vmem_limit_bytes=VMEM_LIMIT),
        name="bucket_rank",
    )(bucket_row)


def _dispatch_kernel(*refs, tile_starts):
    n_src = len(tile_starts) - 1
    zero_ref, pos_ref = refs[:2]
    src_refs = refs[2:2 + n_src]
    dst_ref, zeros_sc, stage_sc, sem, ssem, zsem = refs[2 + n_src:]
    i = pl.program_id(0)
    tt = pos_ref.shape[2]

    def row_block(ref, start, rows):
        return ref.at[pl.ds(pl.multiple_of(start, SUBLANES), rows)]

    def zero_copy(t):
        return pltpu.make_async_copy(zeros_sc, row_block(dst_ref, t * MOE_TILE, MOE_TILE), zsem)

    @pl.when(i == 0)
    def _():
        zeros_sc[...] = jnp.zeros(zeros_sc.shape, F32)
        for wait in (False, True):
            def tile(t, c, wait=wait):
                @pl.when(zero_ref[t] == 1)
                def _():
                    zero_copy(t).wait() if wait else zero_copy(t).start()
                return c
            lax.fori_loop(0, zero_ref.shape[0], tile, 0)

    slot = i % 3
    nxt = (i + 1) % 3
    last = pl.num_programs(0) - 1

    def stage(t, s, wait):
        for k in range(n_src):
            @pl.when((t >= tile_starts[k]) & (t < tile_starts[k + 1]))
            def _(src_ref=src_refs[k], first_tile=tile_starts[k]):
                groups = src_ref.at[pl.ds((t - first_tile) * (tt // SUBLANES), tt // SUBLANES)]
                cp = pltpu.make_async_copy(groups, stage_sc.at[s], ssem.at[s])
                cp.wait() if wait else cp.start()

    def wait_rows(s):
        pltpu.make_async_copy(stage_sc.at[s], stage_sc.at[s], sem.at[s]).wait()

    @pl.when(i == 0)
    def _():
        stage(i, slot, wait=False)

    @pl.when(i >= 2)
    def _():
        wait_rows(nxt)

    @pl.when(i < last)
    def _():
        stage(i + 1, nxt, wait=False)

    stage(i, slot, wait=True)

    def group(it, c):
        for j in range(SUBLANES):
            row = pos_ref[0, 0, it * SUBLANES + j]
            pltpu.make_async_copy(stage_sc.at[slot, it, pl.ds(j, 1)], dst_ref.at[pl.ds(row, 1)], sem.at[slot]).start()
        return c

    lax.fori_loop(0, tt // SUBLANES, group, 0)

    @pl.when((i == last) & (i >= 1))
    def _():
        wait_rows((i + 2) % 3)

    @pl.when(i == last)
    def _():
        wait_rows(slot)


def _dispatch(pos3, sources, zero_mask, n_rows):
    nt, _, tt = pos3.shape
    width = sources[0].shape[1]
    tile_starts = [0]
    for src in sources:
        tile_starts.append(tile_starts[-1] + src.shape[0] // tt)
    any_spec = pl.BlockSpec(memory_space=pl.ANY)
    grid_spec = pltpu.PrefetchScalarGridSpec(
        num_scalar_prefetch=1,
        grid=(nt,),
        in_specs=[pl.BlockSpec((1, 1, tt), lambda i, z: (i, 0, 0), memory_space=pltpu.SMEM)] + [any_spec] * len(sources),
        out_specs=any_spec,
        scratch_shapes=[pltpu.VMEM((MOE_TILE, width), F32), pltpu.VMEM((3, tt // SUBLANES, SUBLANES, width), F32),
                        pltpu.SemaphoreType.DMA((3,)), pltpu.SemaphoreType.DMA((3,)), pltpu.SemaphoreType.DMA(())],
    )
    grouped = [src.reshape(src.shape[0] // SUBLANES, SUBLANES, width) for src in sources]
    return pl.pallas_call(
        functools.partial(_dispatch_kernel, tile_starts=tuple(tile_starts)),
        grid_spec=grid_spec,
        out_shape=jax.ShapeDtypeStruct((n_rows, width), F32),
        compiler_params=_params("arbitrary"),
        name="moe_dispatch",
    )(zero_mask, pos3, *grouped)


def _moe_kernel(ea_ref, eb_ref, valid_ref, h_ref, wga, wua, wda, wgb, wub, wdb, o_ref):
    del ea_ref, eb_ref
    i = pl.program_id(0)
    d = o_ref.shape[1]

    @pl.when(valid_ref[i] == 1)
    def _():
        hb = h_ref[:, :d].astype(BF16)

        def expert(wg, wu, wd):
            hid = _silu(_mm(hb, wg[0, 0])) * _mm(hb, wu[0, 0])
            return _mm(hid.astype(BF16), wd[0, 0])

        ya = expert(wga, wua, wda)
        yb = expert(wgb, wub, wdb)
        o_ref[...] = h_ref[:, d:d + 1] * ya + h_ref[:, d + 1:d + 2] * yb

    @pl.when(valid_ref[i] == 0)
    def _():
        o_ref[...] = jnp.zeros(o_ref.shape, F32)


def _moe(ea, eb, valid, h2s, wg, wu, wd, layer):
    n_tiles = ea.shape[0]
    tm = MOE_TILE
    d = wd.shape[3]
    de = wd.shape[2]
    a_map = lambda i, ea, eb, valid: (layer, ea[i], 0, 0)
    b_map = lambda i, ea, eb, valid: (layer, eb[i], 0, 0)
    row = lambda i, ea, eb, valid: (i, 0)
    grid_spec = pltpu.PrefetchScalarGridSpec(
        num_scalar_prefetch=3,
        grid=(n_tiles,),
        in_specs=[
            pl.BlockSpec((tm, h2s.shape[1]), row),
            pl.BlockSpec((1, 1, d, de), a_map), pl.BlockSpec((1, 1, d, de), a_map), pl.BlockSpec((1, 1, de, d), a_map),
            pl.BlockSpec((1, 1, d, de), b_map), pl.BlockSpec((1, 1, d, de), b_map), pl.BlockSpec((1, 1, de, d), b_map),
        ],
        out_specs=pl.BlockSpec((tm, d), row),
    )
    return pl.pallas_call(
        _moe_kernel,
        grid_spec=grid_spec,
        out_shape=jax.ShapeDtypeStruct((n_tiles * tm, d), F32),
        compiler_params=_params("arbitrary"),
        name="moe_experts",
    )(ea, eb, valid, h2s, wg, wu, wd, wg, wu, wd)


def _combine_kernel(pos_ref, next_pos_ref, x_ref, mod_ref, ys_ref, o_ref, buf, sem):
    i = pl.program_id(0)
    tt = x_ref.shape[0]
    slot = i % 2

    def issue(p_ref, s):
        def group(it, c):
            for j in range(SUBLANES):
                row = p_ref[0, 0, it * SUBLANES + j]
                pltpu.make_async_copy(ys_ref.at[pl.ds(row, 1)], buf.at[s, it, pl.ds(j, 1)], sem.at[s]).start()
            return c
        lax.fori_loop(0, tt // SUBLANES, group, 0)

    @pl.when(i == 0)
    def _():
        issue(pos_ref, 0)

    @pl.when(i + 1 < pl.num_programs(0))
    def _():
        issue(next_pos_ref, 1 - slot)

    pltpu.make_async_copy(buf.at[slot], buf.at[slot], sem.at[slot]).wait()
    o_ref[...] = x_ref[...] + mod_ref[0, 5:6, :] * buf[slot].reshape(tt, x_ref.shape[1])


def _combine(pos3, x1, modl, ys, batch, seq, latent):
    d = x1.shape[1]
    tt = TOKEN_TILE
    nt = seq // tt
    mod_map = (lambda i: (i // nt + 1, 0, 0)) if latent else (lambda i: (0, 0, 0))
    return pl.pallas_call(
        _combine_kernel,
        grid=(batch * nt,),
        in_specs=[
            pl.BlockSpec((1, 1, tt), lambda i: (i, 0, 0), memory_space=pltpu.SMEM),
            pl.BlockSpec((1, 1, tt), lambda i: (jnp.minimum(i + 1, batch * nt - 1), 0, 0), memory_space=pltpu.SMEM),
            pl.BlockSpec((tt, d), lambda i: (i, 0)),
            pl.BlockSpec((1, N_MOD, d), mod_map),
            pl.BlockSpec(memory_space=pl.ANY),
        ],
        out_specs=pl.BlockSpec((tt, d), lambda i: (i, 0)),
        out_shape=jax.ShapeDtypeStruct(x1.shape, F32),
        scratch_shapes=[pltpu.VMEM((2, tt // SUBLANES, SUBLANES, d), F32), pltpu.SemaphoreType.DMA((2,))],
        compiler_params=_params("arbitrary"),
        name="moe_combine_latent" if latent else "moe_combine_ctx",
    )(pos3, pos3, x1, modl, ys)


def _rope_tables(seq):
    rows = seq // GRID_W
    row = jnp.repeat(jnp.arange(rows, dtype=F32), GRID_W)
    col = jnp.tile(jnp.arange(GRID_W, dtype=F32), rows)
    inv = ROPE_THETA ** (-jnp.arange(0, ROPE_AXIS_DIM, 2, dtype=F32) / ROPE_AXIS_DIM)
    cos_parts, sin_parts = [], []
    for pos in (row, col):
        ang = pos[:, None] * inv
        cos_parts += [jnp.cos(ang), jnp.cos(ang)]
        sin_parts += [-jnp.sin(ang), jnp.sin(ang)]
    cos = jnp.concatenate(cos_parts, axis=-1)
    sin = jnp.concatenate(sin_parts, axis=-1)
    return jnp.tile(cos, (1, LANES // HEAD_DIM)), jnp.tile(sin, (1, LANES // HEAD_DIM))


def _tile_tables(start_tiles, n_tiles):
    st = start_tiles[:N_BUCKETS + 1, 0].astype(jnp.int32)
    tiles = jnp.arange(n_tiles, dtype=jnp.int32)
    total = st[N_BUCKETS]
    bucket = jnp.sum((jnp.minimum(tiles, total - 1)[:, None] >= st[None, 1:]).astype(jnp.int32), axis=1)
    bucket = jnp.minimum(bucket, N_BUCKETS - 1)
    ends_bucket = jnp.any((tiles + 1)[:, None] == st[None, 1:], axis=1)
    may_pad = ((tiles >= total) | ends_bucket).astype(jnp.int32)
    pairs = [(a, b) for a in range(EXPERTS_PER_GROUP) for b in range(a + 1, EXPERTS_PER_GROUP)]
    lo = jnp.asarray([p[0] for p in pairs], jnp.int32)
    hi = jnp.asarray([p[1] for p in pairs], jnp.int32)
    grp = bucket // PAIRS_PER_GROUP
    pair = bucket % PAIRS_PER_GROUP
    ea = grp * EXPERTS_PER_GROUP + lo[pair]
    eb = grp * EXPERTS_PER_GROUP + hi[pair]
    return ea, eb, (tiles < total).astype(jnp.int32), may_pad


def _ctx_kv(cache_k_l, cache_v_l):
    b, p = cache_k_l.shape[:2]
    ckt = cache_k_l.transpose(0, 2, 3, 1).astype(BF16)
    v = cache_v_l.transpose(0, 2, 1, 3)
    pad = jnp.zeros((b, N_KV_HEADS, p, LANES - HEAD_DIM), F32).at[..., 0].set(1.0)
    return ckt, jnp.concatenate([v, pad], axis=-1).astype(BF16)


def kernel(x_prompt, x_sample, cache_k, cache_v, c, c_ctx, norm1_g, norm2_g, w_mod, b_mod, w_in, w_fourier,
           q_norm_g, k_norm_g, w_out, w_router_group, b_router_group, w_router_expert, b_router_expert,
           w_gate, w_up, w_down):
    batch, seq, d = x_prompt.shape
    dec_batch, dec_seq, _ = x_sample.shape
    depth = w_in.shape[0]
    streams = [
        dict(latent=False, batch=batch, seq=seq, x=x_prompt.reshape(batch * seq, d)),
        dict(latent=True, batch=dec_batch, seq=dec_seq, x=x_sample.reshape(dec_batch * dec_seq, d)),
    ]
    t_total = sum(s["batch"] * s["seq"] for s in streams)
    n_sorted_tiles = t_total // MOE_TILE + N_BUCKETS

    cond_rows = 2 * SUBLANES
    cs = jnp.zeros((cond_rows, d), F32).at[0].set(c_ctx).at[1:1 + dec_batch].set(c)
    mod = _adaln_mod(cs, w_mod, b_mod).reshape(depth, cond_rows, N_MOD, d)

    seg = np.arange(LANES) // HEAD_DIM
    same_head = (seg[:, None] == seg[None, :]).astype(np.float32) / HEAD_DIM
    bd = jnp.asarray(np.concatenate([same_head, same_head], axis=0), BF16)
    rope_tabs = _rope_tables(dec_seq)
    wg_all, wu_all, wd_all = w_gate.astype(BF16), w_up.astype(BF16), w_down.astype(BF16)
    kc_layers, vc_layers = [], []

    for l in range(depth):
        modl = mod[l]
        w_in_l = w_in[l].astype(BF16)
        w_o_l = w_out[l].astype(BF16)
        qg = jnp.tile(q_norm_g[l], N_Q_HEADS)[None, :]
        kg = jnp.tile(k_norm_g[l], N_KV_HEADS)[None, :]
        n_router = N_EXPERT_GROUPS + N_EXPERTS
        wr = jnp.concatenate([w_router_group[l], w_router_expert[l], jnp.zeros((d, LANES - n_router), F32)], axis=1)
        wr = jnp.concatenate(_split(wr), axis=1)
        br = jnp.concatenate([b_router_group[l], b_router_expert[l], jnp.zeros((LANES - n_router,), F32)])[None, :]

        for s in streams:
            b_, n_, lat = s["batch"], s["seq"], s["latent"]
            res = _inproj(s["x"], modl, norm1_g[l][None, :], w_in_l, qg, kg, bd, rope_tabs, b_, n_, lat)
            f_in, q, kt, vx = res[:4]
            if lat:
                ctx = _ctx_kv(cache_k[:, l], cache_v[:, l])
            else:
                ctx = None
                kc_layers.append(res[4])
                vc_layers.append(res[5])
            fourier = _fourier_fft if _uses_fft(n_) else _fourier_dense
            f_out = fourier(f_in, w_fourier[l].astype(BF16), b_, n_)
            a_out = _attention(q, kt, vx, ctx, b_, n_)
            s["x1"], s["h2e"], s["info"] = _outproj(s["x"], f_out, a_out, modl, norm2_g[l][None, :], w_o_l, wr, br,
                                                    b_, n_, lat)

        bucket_row = jnp.concatenate([s["info"][0:1] for s in streams], axis=1)
        pos, start_tiles = _rank(bucket_row)
        ea, eb, valid, may_pad = _tile_tables(start_tiles, n_sorted_tiles)

        pos3 = pos.reshape(t_total // TOKEN_TILE, 1, TOKEN_TILE)
        h2s = _dispatch(pos3, [s["h2e"] for s in streams], may_pad, n_sorted_tiles * MOE_TILE)
        off = 0
        for s in streams:
            n_tok_tiles = s["batch"] * s["seq"] // TOKEN_TILE
            s["pos3"] = pos3[off:off + n_tok_tiles]
            off += n_tok_tiles
        ys = _moe(ea, eb, valid, h2s, wg_all, wu_all, wd_all, l)
        for s in streams:
            s["x"] = _combine(s["pos3"], s["x1"], modl, ys, s["batch"], s["seq"], s["latent"])

    y_prompt = streams[0]["x"].reshape(batch, seq, d)
    y_sample = streams[1]["x"].reshape(dec_batch, dec_seq, d)
    cache_shape = (batch, depth, seq, N_KV_HEADS, HEAD_DIM)
    new_k = jnp.stack([k.reshape(batch, seq, KV_WIDTH) for k in kc_layers], axis=1).reshape(cache_shape)
    new_v = jnp.stack([v.reshape(batch, seq, KV_WIDTH) for v in vc_layers], axis=1).reshape(cache_shape)
    return (y_prompt, y_sample, new_k, new_v)
```

```python
import functools

import numpy as np
import jax
import jax.numpy as jnp
from jax import lax
from jax.experimental import pallas as pl
from jax.experimental.pallas import tpu as pltpu

F32 = jnp.float32
BF16 = jnp.bfloat16

HEAD_DIM = 64
N_Q_HEADS = 12
N_KV_HEADS = 4
KV_GROUP = N_Q_HEADS // N_KV_HEADS
Q_WIDTH = N_Q_HEADS * HEAD_DIM
GROUP_WIDTH = KV_GROUP * HEAD_DIM
KV_WIDTH = N_KV_HEADS * HEAD_DIM
F_GROUPS = 4
F_GROUP_DIM = 64
F_WIDTH = F_GROUPS * F_GROUP_DIM
GRID_W = 64
ROPE_AXIS_DIM = HEAD_DIM // 2
ROPE_HALF = ROPE_AXIS_DIM // 2
ROPE_THETA = 10000.0
N_EXPERT_GROUPS = 4
EXPERTS_PER_GROUP = 4
N_EXPERTS = N_EXPERT_GROUPS * EXPERTS_PER_GROUP
PAIRS_PER_GROUP = 6
N_BUCKETS = N_EXPERT_GROUPS * PAIRS_PER_GROUP
N_MOD = 6
EPS = 1e-6
ATTN_SCALE = HEAD_DIM ** -0.5
LOG2_E = 1.4426950408889634

LANES = 128
SUBLANES = 8

TOKEN_TILE = 256
SUB_TILE = 256
MOE_TILE = 256
ATTN_Q_TILE = 512
ATTN_K_TILE = 512
ATTN_HEADS_PER_STEP = 1
FFT_TOKEN_TILE = 512
RANK_CHUNK = 512
BUCKET_ROWS = 32
VMEM_LIMIT = 48 * 1024 * 1024
NEG_BIG = -1e30


def _split(x):
    hi = x.astype(BF16)
    lo = (x - hi.astype(F32)).astype(BF16)
    return hi, lo


def _mm(a, b):
    return jnp.dot(a, b, preferred_element_type=F32)


def _mm3(a, b):
    ah, al = _split(a)
    bh, bl = _split(b)
    return _mm(ah, bh) + (_mm(ah, bl) + _mm(al, bh))


def _silu(x):
    return x / (1.0 + jnp.exp(-x))


def _uses_fft(seq):
    return seq > TOKEN_TILE


def _params(*sem):
    return pltpu.CompilerParams(dimension_semantics=sem, vmem_limit_bytes=VMEM_LIMIT)


def _mod_kernel(c_ref, w_ref, b_ref, o_ref):
    o_ref[0] = _mm3(_silu(c_ref[...]), w_ref[0]) + b_ref[0]


def _adaln_mod(cs, w_mod, b_mod):
    n_layers, d, m = w_mod.shape
    r = cs.shape[0]
    tn = d
    return pl.pallas_call(
        _mod_kernel,
        grid=(n_layers, m // tn),
        in_specs=[
            pl.BlockSpec((r, d), lambda l, j: (0, 0)),
            pl.BlockSpec((1, d, tn), lambda l, j: (l, 0, j)),
            pl.BlockSpec((1, 1, tn), lambda l, j: (l, 0, j)),
        ],
        out_specs=pl.BlockSpec((1, r, tn), lambda l, j: (l, 0, j)),
        out_shape=jax.ShapeDtypeStruct((n_layers, r, m), F32),
        compiler_params=_params("arbitrary", "arbitrary"),
        name="adaln_mod",
    )(cs, w_mod, b_mod.reshape(n_layers, 1, m))


def _head_norm(z, gain, bd):
    outs = []
    for j in range(z.shape[1] // LANES):
        zj = z[:, LANES * j:LANES * (j + 1)]
        hi, lo = _split(zj * zj)
        msq = _mm(jnp.concatenate([hi, lo], axis=1), bd)
        outs.append(zj * lax.rsqrt(msq + EPS))
    return jnp.concatenate(outs, axis=1) * gain


def _rope(z, cos, sin):
    lane = lax.broadcasted_iota(jnp.int32, (z.shape[0], LANES), 1)
    first_half = (lane % ROPE_AXIS_DIM) < ROPE_HALF
    outs = []
    for j in range(z.shape[1] // LANES):
        zj = z[:, LANES * j:LANES * (j + 1)]
        partner = jnp.where(first_half, pltpu.roll(zj, LANES - ROPE_HALF, 1), pltpu.roll(zj, ROPE_HALF, 1))
        outs.append(zj * cos + partner * sin)
    return jnp.concatenate(outs, axis=1)


def _inproj_kernel(*refs, latent, grid_major):
    if latent:
        (x_ref, mod_ref, n1_ref, w_ref, qg_ref, kg_ref, bd_ref, cos_ref, sin_ref,
         f_ref, q_ref, kt_ref, vx_ref) = refs
    else:
        (x_ref, mod_ref, n1_ref, w_ref, qg_ref, kg_ref, bd_ref,
         f_ref, q_ref, kt_ref, vx_ref, kc_ref, vc_ref) = refs
    bd = bd_ref[...]
    for r0 in range(0, x_ref.shape[0], SUB_TILE):
        rows = slice(r0, r0 + SUB_TILE)
        x = x_ref[rows, :]
        h = x * lax.rsqrt(jnp.mean(x * x, axis=-1, keepdims=True) + EPS) * n1_ref[...]
        h = h * (1.0 + mod_ref[0, 1:2, :]) + mod_ref[0, 0:1, :]
        p = _mm(h.astype(BF16), w_ref[...])
        if grid_major:
            for a in range(SUB_TILE // GRID_W):
                f_ref[0, :, r0 // GRID_W + a, :] = p[GRID_W * a:GRID_W * (a + 1), :F_WIDTH]
        else:
            f_ref[rows, :] = p[:, :F_WIDTH]
        q = _head_norm(p[:, F_WIDTH:F_WIDTH + Q_WIDTH], qg_ref[...], bd)
        k = _head_norm(p[:, F_WIDTH + Q_WIDTH:F_WIDTH + Q_WIDTH + KV_WIDTH], kg_ref[...], bd)
        v = p[:, F_WIDTH + Q_WIDTH + KV_WIDTH:]
        if latent:
            cos = cos_ref[rows, :]
            sin = sin_ref[rows, :]
            q = _rope(q, cos, sin)
            k = _rope(k, cos, sin)
        else:
            kc_ref[rows, :] = k
            vc_ref[rows, :] = v
        qs = (q * (ATTN_SCALE * LOG2_E)).astype(BF16)
        for g in range(N_KV_HEADS):
            q_ref[g, rows, :] = qs[:, GROUP_WIDTH * g:GROUP_WIDTH * (g + 1)]
        kt_ref[0, :, :, rows] = k.T.reshape(N_KV_HEADS, HEAD_DIM, SUB_TILE).astype(BF16)
        lane = lax.broadcasted_iota(jnp.int32, (SUB_TILE, LANES), 1)
        ones_col = (lane == HEAD_DIM).astype(F32)
        for j in range(KV_WIDTH // LANES):
            vj = v[:, LANES * j:LANES * (j + 1)]
            vx_ref[0, 2 * j, rows, :] = jnp.where(lane < HEAD_DIM, vj, ones_col).astype(BF16)
            vx_ref[0, 2 * j + 1, rows, :] = jnp.where(lane < HEAD_DIM, pltpu.roll(vj, HEAD_DIM, 1),
                                                      ones_col).astype(BF16)


def _inproj(x, modl, n1, w_in, qg, kg, bd, rope_tabs, batch, seq, latent):
    d = x.shape[1]
    grid_major = _uses_fft(seq)
    tt = FFT_TOKEN_TILE if grid_major else TOKEN_TILE
    nt = seq // tt
    in_w = w_in.shape[1]
    row = lambda b, i: (b * nt + i, 0)
    const = lambda b, i: (0, 0)
    mod_map = (lambda b, i: (b + 1, 0, 0)) if latent else (lambda b, i: (0, 0, 0))
    in_specs = [
        pl.BlockSpec((tt, d), row),
        pl.BlockSpec((1, N_MOD, d), mod_map),
        pl.BlockSpec((1, d), const),
        pl.BlockSpec((d, in_w), const),
        pl.BlockSpec((1, Q_WIDTH), const),
        pl.BlockSpec((1, KV_WIDTH), const),
        pl.BlockSpec((2 * LANES, LANES), const),
    ]
    args = [x, modl, n1, w_in, qg, kg, bd]
    if latent:
        in_specs += [pl.BlockSpec((tt, LANES), lambda b, i: (i, 0))] * 2
        args += list(rope_tabs)
    t = batch * seq
    out_shape = [
        jax.ShapeDtypeStruct((batch, GRID_W, seq // GRID_W, F_WIDTH) if grid_major else (t, F_WIDTH), F32),
        jax.ShapeDtypeStruct((N_KV_HEADS, t, GROUP_WIDTH), BF16),
        jax.ShapeDtypeStruct((batch, N_KV_HEADS, HEAD_DIM, seq), BF16),
        jax.ShapeDtypeStruct((batch, N_KV_HEADS, seq, LANES), BF16),
    ]
    out_specs = [
        (pl.BlockSpec((1, GRID_W, tt // GRID_W, F_WIDTH), lambda b, i: (b, 0, i, 0)) if grid_major
         else pl.BlockSpec((tt, F_WIDTH), row)),
        pl.BlockSpec((N_KV_HEADS, tt, GROUP_WIDTH), lambda b, i: (0, b * nt + i, 0)),
        pl.BlockSpec((1, N_KV_HEADS, HEAD_DIM, tt), lambda b, i: (b, 0, 0, i)),
        pl.BlockSpec((1, N_KV_HEADS, tt, LANES), lambda b, i: (b, 0, i, 0)),
    ]
    if not latent:
        out_shape += [jax.ShapeDtypeStruct((t, KV_WIDTH), F32)] * 2
        out_specs += [pl.BlockSpec((tt, KV_WIDTH), row)] * 2
    return pl.pallas_call(
        functools.partial(_inproj_kernel, latent=latent, grid_major=grid_major),
        grid=(batch, nt),
        in_specs=in_specs,
        out_specs=out_specs,
        out_shape=out_shape,
        compiler_params=_params("arbitrary", "arbitrary"),
        name="inproj_latent" if latent else "inproj_ctx",
    )(*args)


def _dft_cos_sin(n):
    idx = np.arange(n, dtype=np.int64)
    ang = 2.0 * np.pi * ((idx[:, None] * idx[None, :]) % n).astype(np.float64) / n
    return np.cos(ang), np.sin(ang)


def _block_diag(m, reps):
    n = m.shape[0]
    out = np.zeros((n * reps, n * reps), m.dtype)
    for g in range(reps):
        out[g * n:(g + 1) * n, g * n:(g + 1) * n] = m
    return out


def _mxu_const(x):
    return jnp.asarray(x, F32).astype(BF16)


def _channel_dft():
    c, s = _dft_cos_sin(F_GROUP_DIM)
    return _mxu_const(_block_diag(c, F_GROUPS)), _mxu_const(_block_diag(s, F_GROUPS))


def _fourier_dense_kernel(z_ref, cn_ref, sn_ref, bdc_ref, bds_ref, wf_ref, o_ref, *, scale):
    z = z_ref[...].astype(BF16)
    zc = _mm(z, bdc_ref[...]).astype(BF16)
    zs = _mm(z, bds_ref[...]).astype(BF16)
    y = (_mm(cn_ref[...], zc) - _mm(sn_ref[...], zs)) * scale
    o_ref[...] = _mm(y.astype(BF16), wf_ref[...])


def _fourier_dense(f_in, w_f, batch, seq):
    cn, sn = _dft_cos_sin(seq)
    bdc, bds = _channel_dft()
    const = lambda b: (0, 0)
    return pl.pallas_call(
        functools.partial(_fourier_dense_kernel, scale=float((seq * F_GROUP_DIM) ** -0.5)),
        grid=(batch,),
        in_specs=[
            pl.BlockSpec((seq, F_WIDTH), lambda b: (b, 0)),
            pl.BlockSpec((seq, seq), const),
            pl.BlockSpec((seq, seq), const),
            pl.BlockSpec((F_WIDTH, F_WIDTH), const),
            pl.BlockSpec((F_WIDTH, F_WIDTH), const),
            pl.BlockSpec((F_WIDTH, F_WIDTH), const),
        ],
        out_specs=pl.BlockSpec((seq, F_WIDTH), lambda b: (b, 0)),
        out_shape=jax.ShapeDtypeStruct((batch * seq, F_WIDTH), F32),
        compiler_params=_params("arbitrary"),
        name="fourier_dense",
    )(f_in, _mxu_const(cn), _mxu_const(sn), bdc, bds, w_f)


def _fft_kernel(z_ref, cs_ref, g_ref, bdc_ref, bds_ref, wf_ref, o_ref, y_sc, *, scale, r, cb):
    def rows(b, carry):
        yb = _mm(cs_ref[...], z_ref[0, b].astype(BF16))
        y_sc[0, :, b, :] = yb[:r]
        y_sc[1, :, b, :] = yb[r:]
        return carry

    lax.fori_loop(0, GRID_W, rows, 0, unroll=8)

    def cols(j, carry):
        xr, xi = [], []
        for cc in range(cb):
            c = j * cb + cc
            ystack = jnp.concatenate([y_sc[0, c], y_sc[1, c]], axis=0).astype(BF16)
            xc = _mm(g_ref[c], ystack)
            xr.append(xc[:GRID_W])
            xi.append(xc[GRID_W:])
        xr = jnp.concatenate(xr, axis=0).astype(BF16)
        xi = jnp.concatenate(xi, axis=0).astype(BF16)
        o = (_mm(xr, bdc_ref[...]) + _mm(xi, bds_ref[...])) * scale
        o = _mm(o.astype(BF16), wf_ref[...])
        for cc in range(cb):
            o_ref[0, :, j * cb + cc, :] = o[GRID_W * cc:GRID_W * (cc + 1)]
        return carry

    lax.fori_loop(0, r // cb, cols, 0, unroll=2)


def _fourier_fft(f_in, w_f, batch, seq):
    r = seq // GRID_W
    cr, sr = _dft_cos_sin(r)
    cs = _mxu_const(np.concatenate([cr, -sr], axis=0))
    bb = np.arange(GRID_W, dtype=np.int64)
    dd = np.arange(GRID_W, dtype=np.int64)
    cc = np.arange(r, dtype=np.int64)
    ang = 2.0 * np.pi * ((bb[None, None, :] * (r * dd[None, :, None] + cc[:, None, None])) % seq) / seq
    gr, gi = np.cos(ang), -np.sin(ang)
    g = _mxu_const(np.concatenate([np.concatenate([gr, -gi], axis=2),
                                   np.concatenate([gi, gr], axis=2)], axis=1))
    bdc, bds = _channel_dft()
    cb = min(r, 8)
    const = lambda b: (0, 0)
    out = pl.pallas_call(
        functools.partial(_fft_kernel, scale=float((seq * F_GROUP_DIM) ** -0.5), r=r, cb=cb),
        grid=(batch,),
        in_specs=[
            pl.BlockSpec((1, GRID_W, r, F_WIDTH), lambda b: (b, 0, 0, 0)),
            pl.BlockSpec((2 * r, r), const),
            pl.BlockSpec((r, 2 * GRID_W, 2 * GRID_W), lambda b: (0, 0, 0)),
            pl.BlockSpec((F_WIDTH, F_WIDTH), const),
            pl.BlockSpec((F_WIDTH, F_WIDTH), const),
            pl.BlockSpec((F_WIDTH, F_WIDTH), const),
        ],
        out_specs=pl.BlockSpec((1, GRID_W, r, F_WIDTH), lambda b: (b, 0, 0, 0)),
        out_shape=jax.ShapeDtypeStruct((batch, GRID_W, r, F_WIDTH), F32),
        scratch_shapes=[pltpu.VMEM((2, r, GRID_W, F_WIDTH), F32)],
        compiler_params=_params("arbitrary"),
        name="fourier_fft",
    )(f_in, cs, g, bdc, bds, w_f)
    return out.reshape(batch * seq, F_WIDTH)


def _attn_kernel(*refs, tk, n_chunks, has_ctx):
    if has_ctx:
        q_ref, kt_ref, vx_ref, ckt_ref, cvx_ref, o_ref = refs
    else:
        q_ref, kt_ref, vx_ref, o_ref = refs
    n_heads, tq = q_ref.shape[:2]
    m_rows = KV_GROUP * tq
    qs = []
    for j in range(n_heads):
        q = q_ref[j]
        qs.append(jnp.concatenate([q[:, HEAD_DIM * h:HEAD_DIM * (h + 1)] for h in range(KV_GROUP)], axis=0))

    def step(qg, kt, vv, carry):
        m, acc = carry
        s = _mm(qg, kt)
        tile_max = functools.reduce(
            jnp.maximum, [s[:, LANES * j:LANES * (j + 1)] for j in range(s.shape[1] // LANES)])
        m_new = jnp.maximum(m, jnp.max(tile_max, axis=-1, keepdims=True))
        p = jnp.exp2(s - m_new).astype(BF16)
        return m_new, acc * jnp.exp2(m - m_new) + _mm(p, vv)

    carries = [(jnp.full((m_rows, 1), NEG_BIG, F32), jnp.zeros((m_rows, LANES), F32)) for _ in range(n_heads)]
    if has_ctx:
        carries = [step(qs[j], ckt_ref[0, j], cvx_ref[0, j], carries[j]) for j in range(n_heads)]
    for i in range(n_chunks):
        lo = tk * i
        carries = [step(qs[j], kt_ref[0, j, :, lo:lo + tk], vx_ref[0, j, lo:lo + tk, :], carries[j])
                   for j in range(n_heads)]
    for j in range(n_heads):
        acc = carries[j][1]
        o = acc[:, :HEAD_DIM] / acc[:, HEAD_DIM:HEAD_DIM + 1]
        o_ref[j] = jnp.concatenate([o[h * tq:(h + 1) * tq] for h in range(KV_GROUP)], axis=1).astype(BF16)


def _attention(q, kt, vx, ctx, batch, seq):
    tq = min(ATTN_Q_TILE, seq)
    tk = min(ATTN_K_TILE, seq)
    nq = seq // tq
    hs = ATTN_HEADS_PER_STEP
    q_spec = pl.BlockSpec((hs, tq, GROUP_WIDTH), lambda b, g, i: (g, b * nq + i, 0))
    in_specs = [
        q_spec,
        pl.BlockSpec((1, hs, HEAD_DIM, seq), lambda b, g, i: (b, g, 0, 0)),
        pl.BlockSpec((1, hs, seq, LANES), lambda b, g, i: (b, g, 0, 0)),
    ]
    args = [q, kt, vx]
    if ctx is not None:
        ckt, cvx = ctx
        past = ckt.shape[-1]
        in_specs += [
            pl.BlockSpec((1, hs, HEAD_DIM, past), lambda b, g, i: (b, g, 0, 0)),
            pl.BlockSpec((1, hs, past, LANES), lambda b, g, i: (b, g, 0, 0)),
        ]
        args += [ckt, cvx]
    return pl.pallas_call(
        functools.partial(_attn_kernel, tk=tk, n_chunks=seq // tk, has_ctx=ctx is not None),
        grid=(batch, N_KV_HEADS // hs, nq),
        in_specs=in_specs,
        out_specs=q_spec,
        out_shape=jax.ShapeDtypeStruct(q.shape, BF16),
        compiler_params=_params("arbitrary", "arbitrary", "arbitrary"),
        name="attention_latent" if ctx is not None else "attention_ctx",
    )(*args)


def _first_index(vals, target):
    idx = jnp.full(target.shape, float(len(vals) - 1), F32)
    for j in range(len(vals) - 2, -1, -1):
        idx = jnp.where(vals[j] == target, float(j), idx)
    return idx


def _route(lt):
    rows = [lt[i:i + 1, :] for i in range(N_EXPERT_GROUPS + N_EXPERTS)]
    gl = rows[:N_EXPERT_GROUPS]
    gmax = functools.reduce(jnp.maximum, gl)
    gidx = _first_index(gl, gmax)
    g_w = 1.0 / functools.reduce(lambda a, b: a + b, [jnp.exp(v - gmax) for v in gl])
    es = []
    for j in range(EXPERTS_PER_GROUP):
        sel = rows[N_EXPERT_GROUPS + (N_EXPERT_GROUPS - 1) * EXPERTS_PER_GROUP + j]
        for g in range(N_EXPERT_GROUPS - 2, -1, -1):
            sel = jnp.where(gidx == float(g), rows[N_EXPERT_GROUPS + g * EXPERTS_PER_GROUP + j], sel)
        es.append(sel)
    e1 = functools.reduce(jnp.maximum, es)
    i1 = _first_index(es, e1)
    rest = [jnp.where(i1 == float(j), -jnp.inf, es[j]) for j in range(EXPERTS_PER_GROUP)]
    e2 = functools.reduce(jnp.maximum, rest)
    i2 = _first_index(rest, e2)
    t = jnp.exp(e2 - e1)
    w1 = g_w / (1.0 + t)
    w2 = w1 * t
    lo = jnp.minimum(i1, i2)
    hi = jnp.maximum(i1, i2)
    w_lo = jnp.where(i1 < i2, w1, w2)
    w_hi = jnp.where(i1 < i2, w2, w1)
    pair = lo * (7.0 - lo) * 0.5 + (hi - lo - 1.0)
    return gidx * float(PAIRS_PER_GROUP) + pair, w_lo, w_hi


def _outproj_kernel(x_ref, f_ref, a_ref, mod_ref, n2_ref, wo_ref, wr_ref, br_ref, x1_ref, h2_ref, info_ref):
    d = x_ref.shape[1]
    for r0 in range(0, x_ref.shape[0], SUB_TILE):
        rows = slice(r0, r0 + SUB_TILE)
        mix = _mm(f_ref[rows, :].astype(BF16), wo_ref[:F_WIDTH, :])
        for g in range(N_KV_HEADS):
            lo = F_WIDTH + GROUP_WIDTH * g
            mix += _mm(a_ref[g, rows, :], wo_ref[lo:lo + GROUP_WIDTH, :])
        x1 = x_ref[rows, :] + mod_ref[0, 2:3, :] * mix
        x1_ref[rows, :] = x1
        h2 = x1 * lax.rsqrt(jnp.mean(x1 * x1, axis=-1, keepdims=True) + EPS) * n2_ref[...]
        h2 = h2 * (1.0 + mod_ref[0, 4:5, :]) + mod_ref[0, 3:4, :]
        h_hi, h_lo = _split(h2)
        both = _mm(h_hi, wr_ref[...])
        logits = both[:, :LANES] + both[:, LANES:] + _mm(h_lo, wr_ref[:, :LANES]) + br_ref[...]
        bucket, w_lo, w_hi = _route(logits.T)
        h2_ref[rows, :d] = h2
        row = lax.broadcasted_iota(jnp.int32, (LANES, SUB_TILE), 0)
        wts = jnp.where(row == 0, w_lo, jnp.where(row == 1, w_hi, 0.0))
        h2_ref[rows, d:] = wts.T
        row8 = lax.broadcasted_iota(jnp.int32, (SUBLANES, SUB_TILE), 0)
        info_ref[:, rows] = jnp.where(row8 == 0, bucket, 0.0)


def _outproj(x, f_out, a_out, modl, n2, w_o, wr, br, batch, seq, latent):
    d = x.shape[1]
    tt = min(seq, 2 * SUB_TILE)
    nt = seq // tt
    t = batch * seq
    row = lambda b, i: (b * nt + i, 0)
    const = lambda b, i: (0, 0)
    mod_map = (lambda b, i: (b + 1, 0, 0)) if latent else (lambda b, i: (0, 0, 0))
    return pl.pallas_call(
        _outproj_kernel,
        grid=(batch, nt),
        in_specs=[
            pl.BlockSpec((tt, d), row),
            pl.BlockSpec((tt, F_WIDTH), row),
            pl.BlockSpec((N_KV_HEADS, tt, GROUP_WIDTH), lambda b, i: (0, b * nt + i, 0)),
            pl.BlockSpec((1, N_MOD, d), mod_map),
            pl.BlockSpec((1, d), const),
            pl.BlockSpec((F_WIDTH + Q_WIDTH, d), const),
            pl.BlockSpec((d, 2 * LANES), const),
            pl.BlockSpec((1, LANES), const),
        ],
        out_specs=[
            pl.BlockSpec((tt, d), row),
            pl.BlockSpec((tt, d + LANES), row),
            pl.BlockSpec((SUBLANES, tt), lambda b, i: (0, b * nt + i)),
        ],
        out_shape=[
            jax.ShapeDtypeStruct((t, d), F32),
            jax.ShapeDtypeStruct((t, d + LANES), F32),
            jax.ShapeDtypeStruct((SUBLANES, t), F32),
        ],
        compiler_params=_params("arbitrary", "arbitrary"),
        name="outproj_latent" if latent else "outproj_ctx",
    )(x, f_out, a_out, modl, n2, w_o, wr, br)


def _rank_kernel(b_ref, pos_ref, start_ref, *, n_chunks, ch):
    rows = lax.broadcasted_iota(jnp.int32, (BUCKET_ROWS, ch), 0).astype(F32)
    ones = jnp.ones((ch, LANES), BF16)
    upper = (lax.broadcasted_iota(jnp.int32, (ch, ch), 0) < lax.broadcasted_iota(jnp.int32, (ch, ch), 1)).astype(BF16)

    def onehot(i):
        off = pl.multiple_of(i * ch, ch)
        return (rows == b_ref[:, pl.ds(off, ch)]).astype(BF16)

    count = lax.fori_loop(0, n_chunks, lambda i, c: c + _mm(onehot(i), ones), jnp.zeros((BUCKET_ROWS, LANES), F32))
    n_tiles = jnp.floor((count + float(MOE_TILE - 1)) * (1.0 / MOE_TILE))
    lower = (lax.broadcasted_iota(jnp.int32, (BUCKET_ROWS, BUCKET_ROWS), 1)
             < lax.broadcasted_iota(jnp.int32, (BUCKET_ROWS, BUCKET_ROWS), 0)).astype(BF16)
    start_tiles = _mm(lower, n_tiles.astype(BF16))
    start_ref[...] = start_tiles
    start_rows = start_tiles * float(MOE_TILE)

    def body(i, carry):
        oh = onehot(i)
        prefix = _mm(oh, upper)
        base = jnp.concatenate([start_rows + carry] * (ch // LANES), axis=1)
        pos = jnp.sum(oh.astype(F32) * (prefix + base), axis=0, keepdims=True)
        off = pl.multiple_of(i * ch, ch)
        pos_ref[:, pl.ds(off, ch)] = pos.astype(jnp.int32)
        return carry + _mm(oh, ones)

    lax.fori_loop(0, n_chunks, body, jnp.zeros((BUCKET_ROWS, LANES), F32))


def _rank(bucket_row):
    t = bucket_row.shape[1]
    ch = RANK_CHUNK
    return pl.pallas_call(
        functools.partial(_rank_kernel, n_chunks=t // ch, ch=ch),
        out_shape=[jax.ShapeDtypeStruct((1, t), jnp.int32), jax.ShapeDtypeStruct((BUCKET_ROWS, LANES), F32)],
        compiler_params=pltpu.CompilerParams(vmem_limit_bytes=VMEM_LIMIT),
        name="bucket_rank",
    )(bucket_row)


def _dispatch_kernel(*refs, tile_starts):
    n_src = len(tile_starts) - 1
    zero_ref, pos_ref = refs[:2]
    src_refs = refs[2:2 + n_src]
    dst_ref, zeros_sc, stage_sc, sem, ssem, zsem = refs[2 + n_src:]
    i = pl.program_id(0)
    tt = pos_ref.shape[2]

    def row_block(ref, start, rows):
        return ref.at[pl.ds(pl.multiple_of(start, SUBLANES), rows)]

    def zero_copy(t):
        return pltpu.make_async_copy(zeros_sc, row_block(dst_ref, t * MOE_TILE, MOE_TILE), zsem)

    @pl.when(i == 0)
    def _():
        zeros_sc[...] = jnp.zeros(zeros_sc.shape, F32)
        for wait in (False, True):
            def tile(t, c, wait=wait):
                @pl.when(zero_ref[t] == 1)
                def _():
                    zero_copy(t).wait() if wait else zero_copy(t).start()
                return c
            lax.fori_loop(0, zero_ref.shape[0], tile, 0)

    slot = i % 3
    nxt = (i + 1) % 3
    last = pl.num_programs(0) - 1

    def stage(t, s, wait):
        for k in range(n_src):
            @pl.when((t >= tile_starts[k]) & (t < tile_starts[k + 1]))
            def _(src_ref=src_refs[k], first_tile=tile_starts[k]):
                groups = src_ref.at[pl.ds((t - first_tile) * (tt // SUBLANES), tt // SUBLANES)]
                cp = pltpu.make_async_copy(groups, stage_sc.at[s], ssem.at[s])
                cp.wait() if wait else cp.start()

    def wait_rows(s):
        pltpu.make_async_copy(stage_sc.at[s], stage_sc.at[s], sem.at[s]).wait()

    @pl.when(i == 0)
    def _():
        stage(i, slot, wait=False)

    @pl.when(i >= 2)
    def _():
        wait_rows(nxt)

    @pl.when(i < last)
    def _():
        stage(i + 1, nxt, wait=False)

    stage(i, slot, wait=True)

    def group(it, c):
        for j in range(SUBLANES):
            row = pos_ref[0, 0, it * SUBLANES + j]
            pltpu.make_async_copy(stage_sc.at[slot, it, pl.ds(j, 1)], dst_ref.at[pl.ds(row, 1)],
                                  sem.at[slot]).start(priority=j % 2)
        return c

    lax.fori_loop(0, tt // SUBLANES, group, 0)

    @pl.when((i == last) & (i >= 1))
    def _():
        wait_rows((i + 2) % 3)

    @pl.when(i == last)
    def _():
        wait_rows(slot)


def _dispatch(pos3, sources, zero_mask, n_rows):
    nt, _, tt = pos3.shape
    width = sources[0].shape[1]
    tile_starts = [0]
    for src in sources:
        tile_starts.append(tile_starts[-1] + src.shape[0] // tt)
    any_spec = pl.BlockSpec(memory_space=pl.ANY)
    grid_spec = pltpu.PrefetchScalarGridSpec(
        num_scalar_prefetch=1,
        grid=(nt,),
        in_specs=[pl.BlockSpec((1, 1, tt), lambda i, z: (i, 0, 0), memory_space=pltpu.SMEM)] + [any_spec] * len(sources),
        out_specs=any_spec,
        scratch_shapes=[pltpu.VMEM((MOE_TILE, width), F32), pltpu.VMEM((3, tt // SUBLANES, SUBLANES, width), F32),
                        pltpu.SemaphoreType.DMA((3,)), pltpu.SemaphoreType.DMA((3,)), pltpu.SemaphoreType.DMA(())],
    )
    grouped = [src.reshape(src.shape[0] // SUBLANES, SUBLANES, width) for src in sources]
    return pl.pallas_call(
        functools.partial(_dispatch_kernel, tile_starts=tuple(tile_starts)),
        grid_spec=grid_spec,
        out_shape=jax.ShapeDtypeStruct((n_rows, width), F32),
        compiler_params=_params("arbitrary"),
        name="moe_dispatch",
    )(zero_mask, pos3, *grouped)


def _moe_kernel(ea_ref, eb_ref, valid_ref, h_ref, wga, wua, wda, wgb, wub, wdb, o_ref):
    del ea_ref, eb_ref
    i = pl.program_id(0)
    d = o_ref.shape[1]

    @pl.when(valid_ref[i] == 1)
    def _():
        hb = h_ref[:, :d].astype(BF16)

        def expert(wg, wu, wd):
            hid = _silu(_mm(hb, wg[0, 0])) * _mm(hb, wu[0, 0])
            return _mm(hid.astype(BF16), wd[0, 0])

        ya = expert(wga, wua, wda)
        yb = expert(wgb, wub, wdb)
        o_ref[...] = h_ref[:, d:d + 1] * ya + h_ref[:, d + 1:d + 2] * yb

    @pl.when(valid_ref[i] == 0)
    def _():
        o_ref[...] = jnp.zeros(o_ref.shape, F32)


def _moe(ea, eb, valid, h2s, wg, wu, wd, layer):
    n_tiles = ea.shape[0]
    tm = MOE_TILE
    d = wd.shape[3]
    de = wd.shape[2]
    a_map = lambda i, ea, eb, valid: (layer, ea[i], 0, 0)
    b_map = lambda i, ea, eb, valid: (layer, eb[i], 0, 0)
    row = lambda i, ea, eb, valid: (i, 0)
    grid_spec = pltpu.PrefetchScalarGridSpec(
        num_scalar_prefetch=3,
        grid=(n_tiles,),
        in_specs=[
            pl.BlockSpec((tm, h2s.shape[1]), row),
            pl.BlockSpec((1, 1, d, de), a_map), pl.BlockSpec((1, 1, d, de), a_map), pl.BlockSpec((1, 1, de, d), a_map),
            pl.BlockSpec((1, 1, d, de), b_map), pl.BlockSpec((1, 1, d, de), b_map), pl.BlockSpec((1, 1, de, d), b_map),
        ],
        out_specs=pl.BlockSpec((tm, d), row),
    )
    return pl.pallas_call(
        _moe_kernel,
        grid_spec=grid_spec,
        out_shape=jax.ShapeDtypeStruct((n_tiles * tm, d), F32),
        compiler_params=_params("arbitrary"),
        name="moe_experts",
    )(ea, eb, valid, h2s, wg, wu, wd, wg, wu, wd)


def _combine_kernel(pos_ref, next_pos_ref, x_ref, mod_ref, ys_ref, o_ref, buf, sem):
    i = pl.program_id(0)
    tt = x_ref.shape[0]
    slot = i % 2

    def issue(p_ref, s):
        def group(it, c):
            for j in range(SUBLANES):
                row = p_ref[0, 0, it * SUBLANES + j]
                pltpu.make_async_copy(ys_ref.at[pl.ds(row, 1)], buf.at[s, it, pl.ds(j, 1)],
                                      sem.at[s]).start(priority=j % 2)
            return c
        lax.fori_loop(0, tt // SUBLANES, group, 0)

    @pl.when(i == 0)
    def _():
        issue(pos_ref, 0)

    @pl.when(i + 1 < pl.num_programs(0))
    def _():
        issue(next_pos_ref, 1 - slot)

    pltpu.make_async_copy(buf.at[slot], buf.at[slot], sem.at[slot]).wait()
    o_ref[...] = x_ref[...] + mod_ref[0, 5:6, :] * buf[slot].reshape(tt, x_ref.shape[1])


def _combine(pos3, x1, modl, ys, batch, seq, latent):
    d = x1.shape[1]
    tt = TOKEN_TILE
    nt = seq // tt
    mod_map = (lambda i: (i // nt + 1, 0, 0)) if latent else (lambda i: (0, 0, 0))
    return pl.pallas_call(
        _combine_kernel,
        grid=(batch * nt,),
        in_specs=[
            pl.BlockSpec((1, 1, tt), lambda i: (i, 0, 0), memory_space=pltpu.SMEM),
            pl.BlockSpec((1, 1, tt), lambda i: (jnp.minimum(i + 1, batch * nt - 1), 0, 0), memory_space=pltpu.SMEM),
            pl.BlockSpec((tt, d), lambda i: (i, 0)),
            pl.BlockSpec((1, N_MOD, d), mod_map),
            pl.BlockSpec(memory_space=pl.ANY),
        ],
        out_specs=pl.BlockSpec((tt, d), lambda i: (i, 0)),
        out_shape=jax.ShapeDtypeStruct(x1.shape, F32),
        scratch_shapes=[pltpu.VMEM((2, tt // SUBLANES, SUBLANES, d), F32), pltpu.SemaphoreType.DMA((2,))],
        compiler_params=_params("arbitrary"),
        name="moe_combine_latent" if latent else "moe_combine_ctx",
    )(pos3, pos3, x1, modl, ys)


def _rope_tables(seq):
    rows = seq // GRID_W
    row = jnp.repeat(jnp.arange(rows, dtype=F32), GRID_W)
    col = jnp.tile(jnp.arange(GRID_W, dtype=F32), rows)
    inv = ROPE_THETA ** (-jnp.arange(0, ROPE_AXIS_DIM, 2, dtype=F32) / ROPE_AXIS_DIM)
    cos_parts, sin_parts = [], []
    for pos in (row, col):
        ang = pos[:, None] * inv
        cos_parts += [jnp.cos(ang), jnp.cos(ang)]
        sin_parts += [-jnp.sin(ang), jnp.sin(ang)]
    cos = jnp.concatenate(cos_parts, axis=-1)
    sin = jnp.concatenate(sin_parts, axis=-1)
    return jnp.tile(cos, (1, LANES // HEAD_DIM)), jnp.tile(sin, (1, LANES // HEAD_DIM))


def _tile_tables(start_tiles, n_tiles):
    st = start_tiles[:N_BUCKETS + 1, 0].astype(jnp.int32)
    tiles = jnp.arange(n_tiles, dtype=jnp.int32)
    total = st[N_BUCKETS]
    bucket = jnp.sum((jnp.minimum(tiles, total - 1)[:, None] >= st[None, 1:]).astype(jnp.int32), axis=1)
    bucket = jnp.minimum(bucket, N_BUCKETS - 1)
    ends_bucket = jnp.any((tiles + 1)[:, None] == st[None, 1:], axis=1)
    may_pad = ((tiles >= total) | ends_bucket).astype(jnp.int32)
    pairs = [(a, b) for a in range(EXPERTS_PER_GROUP) for b in range(a + 1, EXPERTS_PER_GROUP)]
    lo = jnp.asarray([p[0] for p in pairs], jnp.int32)
    hi = jnp.asarray([p[1] for p in pairs], jnp.int32)
    grp = bucket // PAIRS_PER_GROUP
    pair = bucket % PAIRS_PER_GROUP
    ea = grp * EXPERTS_PER_GROUP + lo[pair]
    eb = grp * EXPERTS_PER_GROUP + hi[pair]
    return ea, eb, (tiles < total).astype(jnp.int32), may_pad


def _ctx_kv(cache_k_l, cache_v_l):
    b, p = cache_k_l.shape[:2]
    ckt = cache_k_l.transpose(0, 2, 3, 1).astype(BF16)
    v = cache_v_l.transpose(0, 2, 1, 3)
    pad = jnp.zeros((b, N_KV_HEADS, p, LANES - HEAD_DIM), F32).at[..., 0].set(1.0)
    return ckt, jnp.concatenate([v, pad], axis=-1).astype(BF16)


def kernel(x_prompt, x_sample, cache_k, cache_v, c, c_ctx, norm1_g, norm2_g, w_mod, b_mod, w_in, w_fourier,
           q_norm_g, k_norm_g, w_out, w_router_group, b_router_group, w_router_expert, b_router_expert,
           w_gate, w_up, w_down):
    batch, seq, d = x_prompt.shape
    dec_batch, dec_seq, _ = x_sample.shape
    depth = w_in.shape[0]
    streams = [
        dict(latent=False, batch=batch, seq=seq, x=x_prompt.reshape(batch * seq, d)),
        dict(latent=True, batch=dec_batch, seq=dec_seq, x=x_sample.reshape(dec_batch * dec_seq, d)),
    ]
    t_total = sum(s["batch"] * s["seq"] for s in streams)
    n_sorted_tiles = t_total // MOE_TILE + N_BUCKETS

    cond_rows = 2 * SUBLANES
    cs = jnp.zeros((cond_rows, d), F32).at[0].set(c_ctx).at[1:1 + dec_batch].set(c)
    mod = _adaln_mod(cs, w_mod, b_mod).reshape(depth, cond_rows, N_MOD, d)

    seg = np.arange(LANES) // HEAD_DIM
    same_head = (seg[:, None] == seg[None, :]).astype(np.float32) / HEAD_DIM
    bd = jnp.asarray(np.concatenate([same_head, same_head], axis=0), BF16)
    rope_tabs = _rope_tables(dec_seq)
    wg_all, wu_all, wd_all = w_gate.astype(BF16), w_up.astype(BF16), w_down.astype(BF16)
    kc_layers, vc_layers = [], []

    for l in range(depth):
        modl = mod[l]
        w_in_l = w_in[l].astype(BF16)
        w_o_l = w_out[l].astype(BF16)
        qg = jnp.tile(q_norm_g[l], N_Q_HEADS)[None, :]
        kg = jnp.tile(k_norm_g[l], N_KV_HEADS)[None, :]
        n_router = N_EXPERT_GROUPS + N_EXPERTS
        wr = jnp.concatenate([w_router_group[l], w_router_expert[l], jnp.zeros((d, LANES - n_router), F32)], axis=1)
        wr = jnp.concatenate(_split(wr), axis=1)
        br = jnp.concatenate([b_router_group[l], b_router_expert[l], jnp.zeros((LANES - n_router,), F32)])[None, :]

        for s in streams:
            b_, n_, lat = s["batch"], s["seq"], s["latent"]
            res = _inproj(s["x"], modl, norm1_g[l][None, :], w_in_l, qg, kg, bd, rope_tabs, b_, n_, lat)
            f_in, q, kt, vx = res[:4]
            if lat:
                ctx = _ctx_kv(cache_k[:, l], cache_v[:, l])
            else:
                ctx = None
                kc_layers.append(res[4])
                vc_layers.append(res[5])
            fourier = _fourier_fft if _uses_fft(n_) else _fourier_dense
            f_out = fourier(f_in, w_fourier[l].astype(BF16), b_, n_)
            a_out = _attention(q, kt, vx, ctx, b_, n_)
            s["x1"], s["h2e"], s["info"] = _outproj(s["x"], f_out, a_out, modl, norm2_g[l][None, :], w_o_l, wr, br,
                                                    b_, n_, lat)

        bucket_row = jnp.concatenate([s["info"][0:1] for s in streams], axis=1)
        pos, start_tiles = _rank(bucket_row)
        ea, eb, valid, may_pad = _tile_tables(start_tiles, n_sorted_tiles)

        pos3 = pos.reshape(t_total // TOKEN_TILE, 1, TOKEN_TILE)
        h2s = _dispatch(pos3, [s["h2e"] for s in streams], may_pad, n_sorted_tiles * MOE_TILE)
        off = 0
        for s in streams:
            n_tok_tiles = s["batch"] * s["seq"] // TOKEN_TILE
            s["pos3"] = pos3[off:off + n_tok_tiles]
            off += n_tok_tiles
        ys = _moe(ea, eb, valid, h2s, wg_all, wu_all, wd_all, l)
        for s in streams:
            s["x"] = _combine(s["pos3"], s["x1"], modl, ys, s["batch"], s["seq"], s["latent"])

    y_prompt = streams[0]["x"].reshape(batch, seq, d)
    y_sample = streams[1]["x"].reshape(dec_batch, dec_seq, d)
    cache_shape = (batch, depth, seq, N_KV_HEADS, HEAD_DIM)
    new_k = jnp.stack([k.reshape(batch, seq, KV_WIDTH) for k in kc_layers], axis=1).reshape(cache_shape)
    new_v = jnp.stack([v.reshape(batch, seq, KV_WIDTH) for v in vc_layers], axis=1).reshape(cache_shape)
    return (y_prompt, y_sample, new_k, new_v)
```

```python
import functools

import numpy as np
import jax
import jax.numpy as jnp
from jax import lax
from jax.experimental import pallas as pl
from jax.experimental.pallas import tpu as pltpu

F32 = jnp.float32
BF16 = jnp.bfloat16

HEAD_DIM = 64
N_Q_HEADS = 12
N_KV_HEADS = 4
KV_GROUP = N_Q_HEADS // N_KV_HEADS
Q_WIDTH = N_Q_HEADS * HEAD_DIM
GROUP_WIDTH = KV_GROUP * HEAD_DIM
KV_WIDTH = N_KV_HEADS * HEAD_DIM
F_GROUPS = 4
F_GROUP_DIM = 64
F_WIDTH = F_GROUPS * F_GROUP_DIM
GRID_W = 64
ROPE_AXIS_DIM = HEAD_DIM // 2
ROPE_HALF = ROPE_AXIS_DIM // 2
ROPE_THETA = 10000.0
N_EXPERT_GROUPS = 4
EXPERTS_PER_GROUP = 4
N_EXPERTS = N_EXPERT_GROUPS * EXPERTS_PER_GROUP
PAIRS_PER_GROUP = 6
N_BUCKETS = N_EXPERT_GROUPS * PAIRS_PER_GROUP
N_MOD = 6
EPS = 1e-6
ATTN_SCALE = HEAD_DIM ** -0.5
LOG2_E = 1.4426950408889634

LANES = 128
SUBLANES = 8

TOKEN_TILE = 256
SUB_TILE = 256
MOE_TILE = 256
ATTN_Q_TILE = 1024
ATTN_K_TILE = 512
ATTN_HEADS_PER_STEP = 1
FFT_TOKEN_TILE = 512
RANK_CHUNK = 512
BUCKET_ROWS = 32
VMEM_LIMIT = 48 * 1024 * 1024
NEG_BIG = -1e30


def _split(x):
    hi = x.astype(BF16)
    lo = (x - hi.astype(F32)).astype(BF16)
    return hi, lo


def _mm(a, b):
    return jnp.dot(a, b, preferred_element_type=F32)


def _mm3(a, b):
    ah, al = _split(a)
    bh, bl = _split(b)
    return _mm(ah, bh) + (_mm(ah, bl) + _mm(al, bh))


def _silu(x):
    return x / (1.0 + jnp.exp(-x))


def _uses_fft(seq):
    return seq > TOKEN_TILE


def _params(*sem):
    return pltpu.CompilerParams(dimension_semantics=sem, vmem_limit_bytes=VMEM_LIMIT)


def _mod_kernel(c_ref, w_ref, b_ref, o_ref):
    o_ref[0] = _mm3(_silu(c_ref[...]), w_ref[0]) + b_ref[0]


def _adaln_mod(cs, w_mod, b_mod):
    n_layers, d, m = w_mod.shape
    r = cs.shape[0]
    tn = d
    return pl.pallas_call(
        _mod_kernel,
        grid=(n_layers, m // tn),
        in_specs=[
            pl.BlockSpec((r, d), lambda l, j: (0, 0)),
            pl.BlockSpec((1, d, tn), lambda l, j: (l, 0, j)),
            pl.BlockSpec((1, 1, tn), lambda l, j: (l, 0, j)),
        ],
        out_specs=pl.BlockSpec((1, r, tn), lambda l, j: (l, 0, j)),
        out_shape=jax.ShapeDtypeStruct((n_layers, r, m), F32),
        compiler_params=_params("arbitrary", "arbitrary"),
        name="adaln_mod",
    )(cs, w_mod, b_mod.reshape(n_layers, 1, m))


def _head_norm(z, gain, bd):
    outs = []
    for j in range(z.shape[1] // LANES):
        zj = z[:, LANES * j:LANES * (j + 1)]
        hi, lo = _split(zj * zj)
        msq = _mm(jnp.concatenate([hi, lo], axis=1), bd)
        outs.append(zj * lax.rsqrt(msq + EPS))
    return jnp.concatenate(outs, axis=1) * gain


def _rope(z, cos, sin):
    lane = lax.broadcasted_iota(jnp.int32, (z.shape[0], LANES), 1)
    first_half = (lane % ROPE_AXIS_DIM) < ROPE_HALF
    outs = []
    for j in range(z.shape[1] // LANES):
        zj = z[:, LANES * j:LANES * (j + 1)]
        partner = jnp.where(first_half, pltpu.roll(zj, LANES - ROPE_HALF, 1), pltpu.roll(zj, ROPE_HALF, 1))
        outs.append(zj * cos + partner * sin)
    return jnp.concatenate(outs, axis=1)


def _inproj_kernel(*refs, latent, grid_major):
    if latent:
        (x_ref, mod_ref, n1_ref, w_ref, qg_ref, kg_ref, bd_ref, cos_ref, sin_ref,
         f_ref, q_ref, kt_ref, vx_ref) = refs
    else:
        (x_ref, mod_ref, n1_ref, w_ref, qg_ref, kg_ref, bd_ref,
         f_ref, q_ref, kt_ref, vx_ref, kc_ref, vc_ref) = refs
    bd = bd_ref[...]
    for r0 in range(0, x_ref.shape[0], SUB_TILE):
        rows = slice(r0, r0 + SUB_TILE)
        x = x_ref[rows, :]
        h = x * lax.rsqrt(jnp.mean(x * x, axis=-1, keepdims=True) + EPS) * n1_ref[...]
        h = h * (1.0 + mod_ref[0, 1:2, :]) + mod_ref[0, 0:1, :]
        p = _mm(h.astype(BF16), w_ref[...])
        if grid_major:
            for a in range(SUB_TILE // GRID_W):
                f_ref[0, :, r0 // GRID_W + a, :] = p[GRID_W * a:GRID_W * (a + 1), :F_WIDTH]
        else:
            f_ref[rows, :] = p[:, :F_WIDTH]
        q = _head_norm(p[:, F_WIDTH:F_WIDTH + Q_WIDTH], qg_ref[...], bd)
        k = _head_norm(p[:, F_WIDTH + Q_WIDTH:F_WIDTH + Q_WIDTH + KV_WIDTH], kg_ref[...], bd)
        v = p[:, F_WIDTH + Q_WIDTH + KV_WIDTH:]
        if latent:
            cos = cos_ref[rows, :]
            sin = sin_ref[rows, :]
            q = _rope(q, cos, sin)
            k = _rope(k, cos, sin)
        else:
            kc_ref[rows, :] = k
            vc_ref[rows, :] = v
        qs = (q * (ATTN_SCALE * LOG2_E)).astype(BF16)
        for g in range(N_KV_HEADS):
            q_ref[g, rows, :] = qs[:, GROUP_WIDTH * g:GROUP_WIDTH * (g + 1)]
        kt_ref[0, :, :, rows] = k.T.reshape(N_KV_HEADS, HEAD_DIM, SUB_TILE).astype(BF16)
        lane = lax.broadcasted_iota(jnp.int32, (SUB_TILE, LANES), 1)
        ones_col = (lane == HEAD_DIM).astype(F32)
        for j in range(KV_WIDTH // LANES):
            vj = v[:, LANES * j:LANES * (j + 1)]
            vx_ref[0, 2 * j, rows, :] = jnp.where(lane < HEAD_DIM, vj, ones_col).astype(BF16)
            vx_ref[0, 2 * j + 1, rows, :] = jnp.where(lane < HEAD_DIM, pltpu.roll(vj, HEAD_DIM, 1),
                                                      ones_col).astype(BF16)


def _inproj(x, modl, n1, w_in, qg, kg, bd, rope_tabs, batch, seq, latent):
    d = x.shape[1]
    grid_major = _uses_fft(seq)
    tt = FFT_TOKEN_TILE if grid_major else TOKEN_TILE
    nt = seq // tt
    in_w = w_in.shape[1]
    row = lambda b, i: (b * nt + i, 0)
    const = lambda b, i: (0, 0)
    mod_map = (lambda b, i: (b + 1, 0, 0)) if latent else (lambda b, i: (0, 0, 0))
    in_specs = [
        pl.BlockSpec((tt, d), row),
        pl.BlockSpec((1, N_MOD, d), mod_map),
        pl.BlockSpec((1, d), const),
        pl.BlockSpec((d, in_w), const),
        pl.BlockSpec((1, Q_WIDTH), const),
        pl.BlockSpec((1, KV_WIDTH), const),
        pl.BlockSpec((2 * LANES, LANES), const),
    ]
    args = [x, modl, n1, w_in, qg, kg, bd]
    if latent:
        in_specs += [pl.BlockSpec((tt, LANES), lambda b, i: (i, 0))] * 2
        args += list(rope_tabs)
    t = batch * seq
    out_shape = [
        jax.ShapeDtypeStruct((batch, GRID_W, seq // GRID_W, F_WIDTH) if grid_major else (t, F_WIDTH), F32),
        jax.ShapeDtypeStruct((N_KV_HEADS, t, GROUP_WIDTH), BF16),
        jax.ShapeDtypeStruct((batch, N_KV_HEADS, HEAD_DIM, seq), BF16),
        jax.ShapeDtypeStruct((batch, N_KV_HEADS, seq, LANES), BF16),
    ]
    out_specs = [
        (pl.BlockSpec((1, GRID_W, tt // GRID_W, F_WIDTH), lambda b, i: (b, 0, i, 0)) if grid_major
         else pl.BlockSpec((tt, F_WIDTH), row)),
        pl.BlockSpec((N_KV_HEADS, tt, GROUP_WIDTH), lambda b, i: (0, b * nt + i, 0)),
        pl.BlockSpec((1, N_KV_HEADS, HEAD_DIM, tt), lambda b, i: (b, 0, 0, i)),
        pl.BlockSpec((1, N_KV_HEADS, tt, LANES), lambda b, i: (b, 0, i, 0)),
    ]
    if not latent:
        out_shape += [jax.ShapeDtypeStruct((t, KV_WIDTH), F32)] * 2
        out_specs += [pl.BlockSpec((tt, KV_WIDTH), row)] * 2
    return pl.pallas_call(
        functools.partial(_inproj_kernel, latent=latent, grid_major=grid_major),
        grid=(batch, nt),
        in_specs=in_specs,
        out_specs=out_specs,
        out_shape=out_shape,
        compiler_params=_params("arbitrary", "arbitrary"),
        name="inproj_latent" if latent else "inproj_ctx",
    )(*args)


def _dft_cos_sin(n):
    idx = np.arange(n, dtype=np.int64)
    ang = 2.0 * np.pi * ((idx[:, None] * idx[None, :]) % n).astype(np.float64) / n
    return np.cos(ang), np.sin(ang)


def _block_diag(m, reps):
    n = m.shape[0]
    out = np.zeros((n * reps, n * reps), m.dtype)
    for g in range(reps):
        out[g * n:(g + 1) * n, g * n:(g + 1) * n] = m
    return out


def _mxu_const(x):
    return jnp.asarray(x, F32).astype(BF16)


def _channel_dft():
    c, s = _dft_cos_sin(F_GROUP_DIM)
    return _mxu_const(_block_diag(c, F_GROUPS)), _mxu_const(_block_diag(s, F_GROUPS))


def _fourier_dense_kernel(z_ref, cn_ref, sn_ref, bdc_ref, bds_ref, wf_ref, o_ref, *, scale):
    z = z_ref[...].astype(BF16)
    zc = _mm(z, bdc_ref[...]).astype(BF16)
    zs = _mm(z, bds_ref[...]).astype(BF16)
    y = (_mm(cn_ref[...], zc) - _mm(sn_ref[...], zs)) * scale
    o_ref[...] = _mm(y.astype(BF16), wf_ref[...])


def _fourier_dense(f_in, w_f, batch, seq):
    cn, sn = _dft_cos_sin(seq)
    bdc, bds = _channel_dft()
    const = lambda b: (0, 0)
    return pl.pallas_call(
        functools.partial(_fourier_dense_kernel, scale=float((seq * F_GROUP_DIM) ** -0.5)),
        grid=(batch,),
        in_specs=[
            pl.BlockSpec((seq, F_WIDTH), lambda b: (b, 0)),
            pl.BlockSpec((seq, seq), const),
            pl.BlockSpec((seq, seq), const),
            pl.BlockSpec((F_WIDTH, F_WIDTH), const),
            pl.BlockSpec((F_WIDTH, F_WIDTH), const),
            pl.BlockSpec((F_WIDTH, F_WIDTH), const),
        ],
        out_specs=pl.BlockSpec((seq, F_WIDTH), lambda b: (b, 0)),
        out_shape=jax.ShapeDtypeStruct((batch * seq, F_WIDTH), F32),
        compiler_params=_params("arbitrary"),
        name="fourier_dense",
    )(f_in, _mxu_const(cn), _mxu_const(sn), bdc, bds, w_f)


def _fft_kernel(z_ref, cs_ref, g_ref, bdc_ref, bds_ref, wf_ref, o_ref, y_sc, *, scale, r, cb):
    def rows(b, carry):
        yb = _mm(cs_ref[...], z_ref[0, b].astype(BF16))
        y_sc[0, :, b, :] = yb[:r]
        y_sc[1, :, b, :] = yb[r:]
        return carry

    lax.fori_loop(0, GRID_W, rows, 0, unroll=8)

    def cols(j, carry):
        xr, xi = [], []
        for cc in range(cb):
            c = j * cb + cc
            ystack = jnp.concatenate([y_sc[0, c], y_sc[1, c]], axis=0).astype(BF16)
            xc = _mm(g_ref[c], ystack)
            xr.append(xc[:GRID_W])
            xi.append(xc[GRID_W:])
        xr = jnp.concatenate(xr, axis=0).astype(BF16)
        xi = jnp.concatenate(xi, axis=0).astype(BF16)
        o = (_mm(xr, bdc_ref[...]) + _mm(xi, bds_ref[...])) * scale
        o = _mm(o.astype(BF16), wf_ref[...])
        for cc in range(cb):
            o_ref[0, :, j * cb + cc, :] = o[GRID_W * cc:GRID_W * (cc + 1)]
        return carry

    lax.fori_loop(0, r // cb, cols, 0, unroll=2)


def _fourier_fft(f_in, w_f, batch, seq):
    r = seq // GRID_W
    cr, sr = _dft_cos_sin(r)
    cs = _mxu_const(np.concatenate([cr, -sr], axis=0))
    bb = np.arange(GRID_W, dtype=np.int64)
    dd = np.arange(GRID_W, dtype=np.int64)
    cc = np.arange(r, dtype=np.int64)
    ang = 2.0 * np.pi * ((bb[None, None, :] * (r * dd[None, :, None] + cc[:, None, None])) % seq) / seq
    gr, gi = np.cos(ang), -np.sin(ang)
    g = _mxu_const(np.concatenate([np.concatenate([gr, -gi], axis=2),
                                   np.concatenate([gi, gr], axis=2)], axis=1))
    bdc, bds = _channel_dft()
    cb = min(r, 8)
    const = lambda b: (0, 0)
    out = pl.pallas_call(
        functools.partial(_fft_kernel, scale=float((seq * F_GROUP_DIM) ** -0.5), r=r, cb=cb),
        grid=(batch,),
        in_specs=[
            pl.BlockSpec((1, GRID_W, r, F_WIDTH), lambda b: (b, 0, 0, 0)),
            pl.BlockSpec((2 * r, r), const),
            pl.BlockSpec((r, 2 * GRID_W, 2 * GRID_W), lambda b: (0, 0, 0)),
            pl.BlockSpec((F_WIDTH, F_WIDTH), const),
            pl.BlockSpec((F_WIDTH, F_WIDTH), const),
            pl.BlockSpec((F_WIDTH, F_WIDTH), const),
        ],
        out_specs=pl.BlockSpec((1, GRID_W, r, F_WIDTH), lambda b: (b, 0, 0, 0)),
        out_shape=jax.ShapeDtypeStruct((batch, GRID_W, r, F_WIDTH), F32),
        scratch_shapes=[pltpu.VMEM((2, r, GRID_W, F_WIDTH), F32)],
        compiler_params=_params("arbitrary"),
        name="fourier_fft",
    )(f_in, cs, g, bdc, bds, w_f)
    return out.reshape(batch * seq, F_WIDTH)


def _attn_kernel(*refs, tk, n_chunks, has_ctx):
    if has_ctx:
        q_ref, kt_ref, vx_ref, ckt_ref, cvx_ref, o_ref = refs
    else:
        q_ref, kt_ref, vx_ref, o_ref = refs
    n_heads, tq = q_ref.shape[:2]
    m_rows = KV_GROUP * tq
    qs = []
    for j in range(n_heads):
        q = q_ref[j]
        qs.append(jnp.concatenate([q[:, HEAD_DIM * h:HEAD_DIM * (h + 1)] for h in range(KV_GROUP)], axis=0))

    def step(qg, kt, vv, carry):
        m, acc = carry
        s = _mm(qg, kt)
        tile_max = functools.reduce(
            jnp.maximum, [s[:, LANES * j:LANES * (j + 1)] for j in range(s.shape[1] // LANES)])
        m_new = jnp.maximum(m, jnp.max(tile_max, axis=-1, keepdims=True))
        p = jnp.exp2(s - m_new).astype(BF16)
        return m_new, acc * jnp.exp2(m - m_new) + _mm(p, vv)

    carries = [(jnp.full((m_rows, 1), NEG_BIG, F32), jnp.zeros((m_rows, LANES), F32)) for _ in range(n_heads)]
    if has_ctx:
        carries = [step(qs[j], ckt_ref[0, j], cvx_ref[0, j], carries[j]) for j in range(n_heads)]
    for i in range(n_chunks):
        lo = tk * i
        carries = [step(qs[j], kt_ref[0, j, :, lo:lo + tk], vx_ref[0, j, lo:lo + tk, :], carries[j])
                   for j in range(n_heads)]
    for j in range(n_heads):
        acc = carries[j][1]
        o = acc[:, :HEAD_DIM] / acc[:, HEAD_DIM:HEAD_DIM + 1]
        o_ref[j] = jnp.concatenate([o[h * tq:(h + 1) * tq] for h in range(KV_GROUP)], axis=1).astype(BF16)


def _attention(q, kt, vx, ctx, batch, seq):
    tq = min(ATTN_Q_TILE, seq)
    tk = min(ATTN_K_TILE, seq)
    nq = seq // tq
    hs = ATTN_HEADS_PER_STEP
    q_spec = pl.BlockSpec((hs, tq, GROUP_WIDTH), lambda b, g, i: (g, b * nq + i, 0))
    in_specs = [
        q_spec,
        pl.BlockSpec((1, hs, HEAD_DIM, seq), lambda b, g, i: (b, g, 0, 0)),
        pl.BlockSpec((1, hs, seq, LANES), lambda b, g, i: (b, g, 0, 0)),
    ]
    args = [q, kt, vx]
    if ctx is not None:
        ckt, cvx = ctx
        past = ckt.shape[-1]
        in_specs += [
            pl.BlockSpec((1, hs, HEAD_DIM, past), lambda b, g, i: (b, g, 0, 0)),
            pl.BlockSpec((1, hs, past, LANES), lambda b, g, i: (b, g, 0, 0)),
        ]
        args += [ckt, cvx]
    return pl.pallas_call(
        functools.partial(_attn_kernel, tk=tk, n_chunks=seq // tk, has_ctx=ctx is not None),
        grid=(batch, N_KV_HEADS // hs, nq),
        in_specs=in_specs,
        out_specs=q_spec,
        out_shape=jax.ShapeDtypeStruct(q.shape, BF16),
        compiler_params=_params("arbitrary", "arbitrary", "arbitrary"),
        name="attention_latent" if ctx is not None else "attention_ctx",
    )(*args)


def _first_index(vals, target):
    idx = jnp.full(target.shape, float(len(vals) - 1), F32)
    for j in range(len(vals) - 2, -1, -1):
        idx = jnp.where(vals[j] == target, float(j), idx)
    return idx


def _route(lt):
    rows = [lt[i:i + 1, :] for i in range(N_EXPERT_GROUPS + N_EXPERTS)]
    gl = rows[:N_EXPERT_GROUPS]
    gmax = functools.reduce(jnp.maximum, gl)
    gidx = _first_index(gl, gmax)
    g_w = 1.0 / functools.reduce(lambda a, b: a + b, [jnp.exp(v - gmax) for v in gl])
    es = []
    for j in range(EXPERTS_PER_GROUP):
        sel = rows[N_EXPERT_GROUPS + (N_EXPERT_GROUPS - 1) * EXPERTS_PER_GROUP + j]
        for g in range(N_EXPERT_GROUPS - 2, -1, -1):
            sel = jnp.where(gidx == float(g), rows[N_EXPERT_GROUPS + g * EXPERTS_PER_GROUP + j], sel)
        es.append(sel)
    e1 = functools.reduce(jnp.maximum, es)
    i1 = _first_index(es, e1)
    rest = [jnp.where(i1 == float(j), -jnp.inf, es[j]) for j in range(EXPERTS_PER_GROUP)]
    e2 = functools.reduce(jnp.maximum, rest)
    i2 = _first_index(rest, e2)
    t = jnp.exp(e2 - e1)
    w1 = g_w / (1.0 + t)
    w2 = w1 * t
    lo = jnp.minimum(i1, i2)
    hi = jnp.maximum(i1, i2)
    w_lo = jnp.where(i1 < i2, w1, w2)
    w_hi = jnp.where(i1 < i2, w2, w1)
    pair = lo * (7.0 - lo) * 0.5 + (hi - lo - 1.0)
    return gidx * float(PAIRS_PER_GROUP) + pair, w_lo, w_hi


def _outproj_kernel(x_ref, f_ref, a_ref, mod_ref, n2_ref, wo_ref, wr_ref, br_ref, x1_ref, h2_ref, info_ref):
    d = x_ref.shape[1]
    for r0 in range(0, x_ref.shape[0], SUB_TILE):
        rows = slice(r0, r0 + SUB_TILE)
        mix = _mm(f_ref[rows, :].astype(BF16), wo_ref[:F_WIDTH, :])
        for g in range(N_KV_HEADS):
            lo = F_WIDTH + GROUP_WIDTH * g
            mix += _mm(a_ref[g, rows, :], wo_ref[lo:lo + GROUP_WIDTH, :])
        x1 = x_ref[rows, :] + mod_ref[0, 2:3, :] * mix
        x1_ref[rows, :] = x1
        h2 = x1 * lax.rsqrt(jnp.mean(x1 * x1, axis=-1, keepdims=True) + EPS) * n2_ref[...]
        h2 = h2 * (1.0 + mod_ref[0, 4:5, :]) + mod_ref[0, 3:4, :]
        h_hi, h_lo = _split(h2)
        both = _mm(h_hi, wr_ref[...])
        logits = both[:, :LANES] + both[:, LANES:] + _mm(h_lo, wr_ref[:, :LANES]) + br_ref[...]
        bucket, w_lo, w_hi = _route(logits.T)
        h2_ref[rows, :d] = h2
        row = lax.broadcasted_iota(jnp.int32, (LANES, SUB_TILE), 0)
        wts = jnp.where(row == 0, w_lo, jnp.where(row == 1, w_hi, 0.0))
        h2_ref[rows, d:] = wts.T
        row8 = lax.broadcasted_iota(jnp.int32, (SUBLANES, SUB_TILE), 0)
        info_ref[:, rows] = jnp.where(row8 == 0, bucket, 0.0)


def _outproj(x, f_out, a_out, modl, n2, w_o, wr, br, batch, seq, latent):
    d = x.shape[1]
    tt = min(seq, 2 * SUB_TILE)
    nt = seq // tt
    t = batch * seq
    row = lambda b, i: (b * nt + i, 0)
    const = lambda b, i: (0, 0)
    mod_map = (lambda b, i: (b + 1, 0, 0)) if latent else (lambda b, i: (0, 0, 0))
    return pl.pallas_call(
        _outproj_kernel,
        grid=(batch, nt),
        in_specs=[
            pl.BlockSpec((tt, d), row),
            pl.BlockSpec((tt, F_WIDTH), row),
            pl.BlockSpec((N_KV_HEADS, tt, GROUP_WIDTH), lambda b, i: (0, b * nt + i, 0)),
            pl.BlockSpec((1, N_MOD, d), mod_map),
            pl.BlockSpec((1, d), const),
            pl.BlockSpec((F_WIDTH + Q_WIDTH, d), const),
            pl.BlockSpec((d, 2 * LANES), const),
            pl.BlockSpec((1, LANES), const),
        ],
        out_specs=[
            pl.BlockSpec((tt, d), row),
            pl.BlockSpec((tt, d + LANES), row),
            pl.BlockSpec((SUBLANES, tt), lambda b, i: (0, b * nt + i)),
        ],
        out_shape=[
            jax.ShapeDtypeStruct((t, d), F32),
            jax.ShapeDtypeStruct((t, d + LANES), F32),
            jax.ShapeDtypeStruct((SUBLANES, t), F32),
        ],
        compiler_params=_params("arbitrary", "arbitrary"),
        name="outproj_latent" if latent else "outproj_ctx",
    )(x, f_out, a_out, modl, n2, w_o, wr, br)


def _rank_kernel(b_ref, pos_ref, start_ref, *, n_chunks, ch):
    rows = lax.broadcasted_iota(jnp.int32, (BUCKET_ROWS, ch), 0).astype(F32)
    ones = jnp.ones((ch, LANES), BF16)
    upper = (lax.broadcasted_iota(jnp.int32, (ch, ch), 0) < lax.broadcasted_iota(jnp.int32, (ch, ch), 1)).astype(BF16)

    def onehot(i):
        off = pl.multiple_of(i * ch, ch)
        return (rows == b_ref[:, pl.ds(off, ch)]).astype(BF16)

    count = lax.fori_loop(0, n_chunks, lambda i, c: c + _mm(onehot(i), ones), jnp.zeros((BUCKET_ROWS, LANES), F32))
    n_tiles = jnp.floor((count + float(MOE_TILE - 1)) * (1.0 / MOE_TILE))
    lower = (lax.broadcasted_iota(jnp.int32, (BUCKET_ROWS, BUCKET_ROWS), 1)
             < lax.broadcasted_iota(jnp.int32, (BUCKET_ROWS, BUCKET_ROWS), 0)).astype(BF16)
    start_tiles = _mm(lower, n_tiles.astype(BF16))
    start_ref[...] = start_tiles
    start_rows = start_tiles * float(MOE_TILE)

    def body(i, carry):
        oh = onehot(i)
        prefix = _mm(oh, upper)
        base = jnp.concatenate([start_rows + carry] * (ch // LANES), axis=1)
        pos = jnp.sum(oh.astype(F32) * (prefix + base), axis=0, keepdims=True)
        off = pl.multiple_of(i * ch, ch)
        pos_ref[:, pl.ds(off, ch)] = pos.astype(jnp.int32)
        return carry + _mm(oh, ones)

    lax.fori_loop(0, n_chunks, body, jnp.zeros((BUCKET_ROWS, LANES), F32))


def _rank(bucket_row):
    t = bucket_row.shape[1]
    ch = RANK_CHUNK
    return pl.pallas_call(
        functools.partial(_rank_kernel, n_chunks=t // ch, ch=ch),
        out_shape=[jax.ShapeDtypeStruct((1, t), jnp.int32), jax.ShapeDtypeStruct((BUCKET_ROWS, LANES), F32)],
        compiler_params=pltpu.CompilerParams(vmem_limit_bytes=VMEM_LIMIT),
        name="bucket_rank",
    )(bucket_row)


def _dispatch_kernel(*refs, tile_starts):
    n_src = len(tile_starts) - 1
    zero_ref, pos_ref = refs[:2]
    src_refs = refs[2:2 + n_src]
    dst_ref, zeros_sc, stage_sc, sem, ssem, zsem = refs[2 + n_src:]
    i = pl.program_id(0)
    tt = pos_ref.shape[2]

    def row_block(ref, start, rows):
        return ref.at[pl.ds(pl.multiple_of(start, SUBLANES), rows)]

    def zero_copy(t):
        return pltpu.make_async_copy(zeros_sc, row_block(dst_ref, t * MOE_TILE, MOE_TILE), zsem)

    @pl.when(i == 0)
    def _():
        zeros_sc[...] = jnp.zeros(zeros_sc.shape, F32)
        for wait in (False, True):
            def tile(t, c, wait=wait):
                @pl.when(zero_ref[t] == 1)
                def _():
                    zero_copy(t).wait() if wait else zero_copy(t).start()
                return c
            lax.fori_loop(0, zero_ref.shape[0], tile, 0)

    slot = i % 3
    nxt = (i + 1) % 3
    last = pl.num_programs(0) - 1

    def stage(t, s, wait):
        for k in range(n_src):
            @pl.when((t >= tile_starts[k]) & (t < tile_starts[k + 1]))
            def _(src_ref=src_refs[k], first_tile=tile_starts[k]):
                groups = src_ref.at[pl.ds((t - first_tile) * (tt // SUBLANES), tt // SUBLANES)]
                cp = pltpu.make_async_copy(groups, stage_sc.at[s], ssem.at[s])
                cp.wait() if wait else cp.start()

    def wait_rows(s):
        pltpu.make_async_copy(stage_sc.at[s], stage_sc.at[s], sem.at[s]).wait()

    @pl.when(i == 0)
    def _():
        stage(i, slot, wait=False)

    @pl.when(i >= 2)
    def _():
        wait_rows(nxt)

    @pl.when(i < last)
    def _():
        stage(i + 1, nxt, wait=False)

    stage(i, slot, wait=True)

    def group(it, c):
        for j in range(SUBLANES):
            row = pos_ref[0, 0, it * SUBLANES + j]
            pltpu.make_async_copy(stage_sc.at[slot, it, pl.ds(j, 1)], dst_ref.at[pl.ds(row, 1)],
                                  sem.at[slot]).start(priority=j % 2)
        return c

    lax.fori_loop(0, tt // SUBLANES, group, 0)

    @pl.when((i == last) & (i >= 1))
    def _():
        wait_rows((i + 2) % 3)

    @pl.when(i == last)
    def _():
        wait_rows(slot)


def _dispatch(pos3, sources, zero_mask, n_rows):
    nt, _, tt = pos3.shape
    width = sources[0].shape[1]
    tile_starts = [0]
    for src in sources:
        tile_starts.append(tile_starts[-1] + src.shape[0] // tt)
    any_spec = pl.BlockSpec(memory_space=pl.ANY)
    grid_spec = pltpu.PrefetchScalarGridSpec(
        num_scalar_prefetch=1,
        grid=(nt,),
        in_specs=[pl.BlockSpec((1, 1, tt), lambda i, z: (i, 0, 0), memory_space=pltpu.SMEM)] + [any_spec] * len(sources),
        out_specs=any_spec,
        scratch_shapes=[pltpu.VMEM((MOE_TILE, width), F32), pltpu.VMEM((3, tt // SUBLANES, SUBLANES, width), F32),
                        pltpu.SemaphoreType.DMA((3,)), pltpu.SemaphoreType.DMA((3,)), pltpu.SemaphoreType.DMA(())],
    )
    grouped = [src.reshape(src.shape[0] // SUBLANES, SUBLANES, width) for src in sources]
    return pl.pallas_call(
        functools.partial(_dispatch_kernel, tile_starts=tuple(tile_starts)),
        grid_spec=grid_spec,
        out_shape=jax.ShapeDtypeStruct((n_rows, width), F32),
        compiler_params=_params("arbitrary"),
        name="moe_dispatch",
    )(zero_mask, pos3, *grouped)


def _moe_kernel(ea_ref, eb_ref, valid_ref, h_ref, wga, wua, wda, wgb, wub, wdb, o_ref):
    del ea_ref, eb_ref
    i = pl.program_id(0)
    d = o_ref.shape[1]

    @pl.when(valid_ref[i] == 1)
    def _():
        hb = h_ref[:, :d].astype(BF16)

        def expert(wg, wu, wd):
            hid = _silu(_mm(hb, wg[0, 0])) * _mm(hb, wu[0, 0])
            return _mm(hid.astype(BF16), wd[0, 0])

        ya = expert(wga, wua, wda)
        yb = expert(wgb, wub, wdb)
        o_ref[...] = h_ref[:, d:d + 1] * ya + h_ref[:, d + 1:d + 2] * yb

    @pl.when(valid_ref[i] == 0)
    def _():
        o_ref[...] = jnp.zeros(o_ref.shape, F32)


def _moe(ea, eb, valid, h2s, wg, wu, wd, layer):
    n_tiles = ea.shape[0]
    tm = MOE_TILE
    d = wd.shape[3]
    de = wd.shape[2]
    a_map = lambda i, ea, eb, valid: (layer, ea[i], 0, 0)
    b_map = lambda i, ea, eb, valid: (layer, eb[i], 0, 0)
    row = lambda i, ea, eb, valid: (i, 0)
    grid_spec = pltpu.PrefetchScalarGridSpec(
        num_scalar_prefetch=3,
        grid=(n_tiles,),
        in_specs=[
            pl.BlockSpec((tm, h2s.shape[1]), row),
            pl.BlockSpec((1, 1, d, de), a_map), pl.BlockSpec((1, 1, d, de), a_map), pl.BlockSpec((1, 1, de, d), a_map),
            pl.BlockSpec((1, 1, d, de), b_map), pl.BlockSpec((1, 1, d, de), b_map), pl.BlockSpec((1, 1, de, d), b_map),
        ],
        out_specs=pl.BlockSpec((tm, d), row),
    )
    return pl.pallas_call(
        _moe_kernel,
        grid_spec=grid_spec,
        out_shape=jax.ShapeDtypeStruct((n_tiles * tm, d), F32),
        compiler_params=_params("arbitrary"),
        name="moe_experts",
    )(ea, eb, valid, h2s, wg, wu, wd, wg, wu, wd)


def _combine_kernel(pos_ref, next_pos_ref, x_ref, mod_ref, ys_ref, o_ref, buf, sem):
    i = pl.program_id(0)
    tt = x_ref.shape[0]
    slot = i % 2

    def issue(p_ref, s):
        def group(it, c):
            for j in range(SUBLANES):
                row = p_ref[0, 0, it * SUBLANES + j]
                pltpu.make_async_copy(ys_ref.at[pl.ds(row, 1)], buf.at[s, it, pl.ds(j, 1)],
                                      sem.at[s]).start(priority=j % 2)
            return c
        lax.fori_loop(0, tt // SUBLANES, group, 0)

    @pl.when(i == 0)
    def _():
        issue(pos_ref, 0)

    @pl.when(i + 1 < pl.num_programs(0))
    def _():
        issue(next_pos_ref, 1 - slot)

    pltpu.make_async_copy(buf.at[slot], buf.at[slot], sem.at[slot]).wait()
    o_ref[...] = x_ref[...] + mod_ref[0, 5:6, :] * buf[slot].reshape(tt, x_ref.shape[1])


def _combine(pos3, x1, modl, ys, batch, seq, latent):
    d = x1.shape[1]
    tt = TOKEN_TILE
    nt = seq // tt
    mod_map = (lambda i: (i // nt + 1, 0, 0)) if latent else (lambda i: (0, 0, 0))
    return pl.pallas_call(
        _combine_kernel,
        grid=(batch * nt,),
        in_specs=[
            pl.BlockSpec((1, 1, tt), lambda i: (i, 0, 0), memory_space=pltpu.SMEM),
            pl.BlockSpec((1, 1, tt), lambda i: (jnp.minimum(i + 1, batch * nt - 1), 0, 0), memory_space=pltpu.SMEM),
            pl.BlockSpec((tt, d), lambda i: (i, 0)),
            pl.BlockSpec((1, N_MOD, d), mod_map),
            pl.BlockSpec(memory_space=pl.ANY),
        ],
        out_specs=pl.BlockSpec((tt, d), lambda i: (i, 0)),
        out_shape=jax.ShapeDtypeStruct(x1.shape, F32),
        scratch_shapes=[pltpu.VMEM((2, tt // SUBLANES, SUBLANES, d), F32), pltpu.SemaphoreType.DMA((2,))],
        compiler_params=_params("arbitrary"),
        name="moe_combine_latent" if latent else "moe_combine_ctx",
    )(pos3, pos3, x1, modl, ys)


def _rope_tables(seq):
    rows = seq // GRID_W
    row = jnp.repeat(jnp.arange(rows, dtype=F32), GRID_W)
    col = jnp.tile(jnp.arange(GRID_W, dtype=F32), rows)
    inv = ROPE_THETA ** (-jnp.arange(0, ROPE_AXIS_DIM, 2, dtype=F32) / ROPE_AXIS_DIM)
    cos_parts, sin_parts = [], []
    for pos in (row, col):
        ang = pos[:, None] * inv
        cos_parts += [jnp.cos(ang), jnp.cos(ang)]
        sin_parts += [-jnp.sin(ang), jnp.sin(ang)]
    cos = jnp.concatenate(cos_parts, axis=-1)
    sin = jnp.concatenate(sin_parts, axis=-1)
    return jnp.tile(cos, (1, LANES // HEAD_DIM)), jnp.tile(sin, (1, LANES // HEAD_DIM))


def _tile_tables(start_tiles, n_tiles):
    st = start_tiles[:N_BUCKETS + 1, 0].astype(jnp.int32)
    tiles = jnp.arange(n_tiles, dtype=jnp.int32)
    total = st[N_BUCKETS]
    bucket = jnp.sum((jnp.minimum(tiles, total - 1)[:, None] >= st[None, 1:]).astype(jnp.int32), axis=1)
    bucket = jnp.minimum(bucket, N_BUCKETS - 1)
    ends_bucket = jnp.any((tiles + 1)[:, None] == st[None, 1:], axis=1)
    may_pad = ((tiles >= total) | ends_bucket).astype(jnp.int32)
    pairs = [(a, b) for a in range(EXPERTS_PER_GROUP) for b in range(a + 1, EXPERTS_PER_GROUP)]
    lo = jnp.asarray([p[0] for p in pairs], jnp.int32)
    hi = jnp.asarray([p[1] for p in pairs], jnp.int32)
    grp = bucket // PAIRS_PER_GROUP
    pair = bucket % PAIRS_PER_GROUP
    ea = grp * EXPERTS_PER_GROUP + lo[pair]
    eb = grp * EXPERTS_PER_GROUP + hi[pair]
    return ea, eb, (tiles < total).astype(jnp.int32), may_pad


def _ctx_kv(cache_k_l, cache_v_l):
    b, p = cache_k_l.shape[:2]
    ckt = cache_k_l.transpose(0, 2, 3, 1).astype(BF16)
    v = cache_v_l.transpose(0, 2, 1, 3)
    pad = jnp.zeros((b, N_KV_HEADS, p, LANES - HEAD_DIM), F32).at[..., 0].set(1.0)
    return ckt, jnp.concatenate([v, pad], axis=-1).astype(BF16)


def kernel(x_prompt, x_sample, cache_k, cache_v, c, c_ctx, norm1_g, norm2_g, w_mod, b_mod, w_in, w_fourier,
           q_norm_g, k_norm_g, w_out, w_router_group, b_router_group, w_router_expert, b_router_expert,
           w_gate, w_up, w_down):
    batch, seq, d = x_prompt.shape
    dec_batch, dec_seq, _ = x_sample.shape
    depth = w_in.shape[0]
    streams = [
        dict(latent=False, batch=batch, seq=seq, x=x_prompt.reshape(batch * seq, d)),
        dict(latent=True, batch=dec_batch, seq=dec_seq, x=x_sample.reshape(dec_batch * dec_seq, d)),
    ]
    t_total = sum(s["batch"] * s["seq"] for s in streams)
    n_sorted_tiles = t_total // MOE_TILE + N_BUCKETS

    cond_rows = 2 * SUBLANES
    cs = jnp.zeros((cond_rows, d), F32).at[0].set(c_ctx).at[1:1 + dec_batch].set(c)
    mod = _adaln_mod(cs, w_mod, b_mod).reshape(depth, cond_rows, N_MOD, d)

    seg = np.arange(LANES) // HEAD_DIM
    same_head = (seg[:, None] == seg[None, :]).astype(np.float32) / HEAD_DIM
    bd = jnp.asarray(np.concatenate([same_head, same_head], axis=0), BF16)
    rope_tabs = _rope_tables(dec_seq)
    wg_all, wu_all, wd_all = w_gate.astype(BF16), w_up.astype(BF16), w_down.astype(BF16)
    kc_layers, vc_layers = [], []

    for l in range(depth):
        modl = mod[l]
        w_in_l = w_in[l].astype(BF16)
        w_o_l = w_out[l].astype(BF16)
        qg = jnp.tile(q_norm_g[l], N_Q_HEADS)[None, :]
        kg = jnp.tile(k_norm_g[l], N_KV_HEADS)[None, :]
        n_router = N_EXPERT_GROUPS + N_EXPERTS
        wr = jnp.concatenate([w_router_group[l], w_router_expert[l], jnp.zeros((d, LANES - n_router), F32)], axis=1)
        wr = jnp.concatenate(_split(wr), axis=1)
        br = jnp.concatenate([b_router_group[l], b_router_expert[l], jnp.zeros((LANES - n_router,), F32)])[None, :]

        for s in streams:
            b_, n_, lat = s["batch"], s["seq"], s["latent"]
            res = _inproj(s["x"], modl, norm1_g[l][None, :], w_in_l, qg, kg, bd, rope_tabs, b_, n_, lat)
            f_in, q, kt, vx = res[:4]
            if lat:
                ctx = _ctx_kv(cache_k[:, l], cache_v[:, l])
            else:
                ctx = None
                kc_layers.append(res[4])
                vc_layers.append(res[5])
            fourier = _fourier_fft if _uses_fft(n_) else _fourier_dense
            f_out = fourier(f_in, w_fourier[l].astype(BF16), b_, n_)
            a_out = _attention(q, kt, vx, ctx, b_, n_)
            s["x1"], s["h2e"], s["info"] = _outproj(s["x"], f_out, a_out, modl, norm2_g[l][None, :], w_o_l, wr, br,
                                                    b_, n_, lat)

        bucket_row = jnp.concatenate([s["info"][0:1] for s in streams], axis=1)
        pos, start_tiles = _rank(bucket_row)
        ea, eb, valid, may_pad = _tile_tables(start_tiles, n_sorted_tiles)

        pos3 = pos.reshape(t_total // TOKEN_TILE, 1, TOKEN_TILE)
        h2s = _dispatch(pos3, [s["h2e"] for s in streams], may_pad, n_sorted_tiles * MOE_TILE)
        off = 0
        for s in streams:
            n_tok_tiles = s["batch"] * s["seq"] // TOKEN_TILE
            s["pos3"] = pos3[off:off + n_tok_tiles]
            off += n_tok_tiles
        ys = _moe(ea, eb, valid, h2s, wg_all, wu_all, wd_all, l)
        for s in streams:
            s["x"] = _combine(s["pos3"], s["x1"], modl, ys, s["batch"], s["seq"], s["latent"])

    y_prompt = streams[0]["x"].reshape(batch, seq, d)
    y_sample = streams[1]["x"].reshape(dec_batch, dec_seq, d)
    cache_shape = (batch, depth, seq, N_KV_HEADS, HEAD_DIM)
    new_k = jnp.stack([k.reshape(batch, seq, KV_WIDTH) for k in kc_layers], axis=1).reshape(cache_shape)
    new_v = jnp.stack([v.reshape(batch, seq, KV_WIDTH) for v in vc_layers], axis=1).reshape(cache_shape)
    return (y_prompt, y_sample, new_k, new_v)
```
